```python
import math
import jax, jax.numpy as jnp
from jax import lax
import numpy as np


D_MODEL = 1024
BATCH = 16
SEQ = 2048
DEPTH = 4

N_META = 16
CHUNK = 64
NEG = -1e30
LN_EPS = 1e-5
DN_ALPHA = (2.0 * DEPTH) ** 0.25
DN_BETA = (8.0 * DEPTH) ** -0.25

A_HEADS = 4
A_DQK = D_MODEL // 16
A_DV = D_MODEL // 8
A_Q = A_HEADS * A_DQK
A_V = A_HEADS * A_DV
A_GATE_CAP = 15.0
B_CH = D_MODEL // 2
B_GROUP = 16
B_GROUPS = B_CH // B_GROUP
B_STATE = 64
C_HEADS = D_MODEL // 128
C_DK = 128
C_DV = 128
C_W = C_HEADS * C_DK
C_CONV = 4
N_EXPERTS = 32
TOP_K = 4
D_EXPERT = D_MODEL
SWIGLU_LIMIT = 7.0
SWIGLU_ALPHA = 1.702
MOE_BLOCK = 256

N_EVEN = (DEPTH + 1) // 2
N_ODD = DEPTH // 2
EVEN_IN = 2 * A_Q + 2 * A_V + 2 * A_HEADS + B_CH
ODD_IN = 4 * C_W + 2 * C_HEADS

kernel_name = 'hybrid_mlstm_s5_gdn_moe_deepnorm'


def _layer_norm(x, g, b):
    xf = x.astype(jnp.float32)
    mu = xf.mean(-1, keepdims=True)
    var = jnp.square(xf - mu).mean(-1, keepdims=True)
    return ((xf - mu) * lax.rsqrt(var + LN_EPS) * g.astype(jnp.float32) + b.astype(jnp.float32)).astype(x.dtype)


def _l2norm(x):
    return x * lax.rsqrt(jnp.sum(x * x, axis=-1, keepdims=True) + 1e-6)


def _pad_front(t, n, value):
    pad = [(0, 0)] * t.ndim
    pad[1] = (n, 0)
    return jnp.pad(t, pad, constant_values=value)


def _split_chunks(t, n_chunks):
    t = t.reshape((t.shape[0], n_chunks, CHUNK) + t.shape[2:])
    return jnp.moveaxis(t, 3, 1)


def _merge_chunks(t):
    t = jnp.moveaxis(t, 1, 3)
    return t.reshape((t.shape[0], t.shape[1] * t.shape[2]) + t.shape[3:])


def _mlstm_chunkwise(q, k, v, i_pre, logf):
    pad = CHUNK - N_META
    q, k, v = (_pad_front(t, pad, 0.0) for t in (q, k, v))
    i_pre = _pad_front(i_pre, pad, NEG)
    logf = _pad_front(logf, pad, 0.0)
    nc = q.shape[1] // CHUNK
    qc, kc, vc, ic, fc = (_split_chunks(t, nc) for t in (q, k, v, i_pre, logf))
    b = jnp.cumsum(fc, axis=-1)
    b_last = b[..., -1]
    a_end = b_last[..., None] - b + ic
    m_loc = a_end.max(-1)
    w_end = jnp.exp(a_end - m_loc[..., None])
    d_c = jnp.einsum('bhnc,bhnck,bhncv->bhnkv', w_end, kc, vc)
    d_n = jnp.einsum('bhnc,bhnck->bhnk', w_end, kc)

    def step(carry, inp):
        c_st, n_st, m_st = carry
        dc_i, dn_i, ml_i, bl_i = inp
        m_new = jnp.maximum(bl_i + m_st, ml_i)
        s_old = jnp.exp(bl_i + m_st - m_new)
        s_new = jnp.exp(ml_i - m_new)
        c2 = s_old[..., None, None] * c_st + s_new[..., None, None] * dc_i
        n2 = s_old[..., None] * n_st + s_new[..., None] * dn_i
        return (c2, n2, m_new), (c_st, n_st, m_st)

    bsz = q.shape[0]
    init = (jnp.zeros((bsz, A_HEADS, A_DQK, A_DV), jnp.float32),
            jnp.zeros((bsz, A_HEADS, A_DQK), jnp.float32),
            jnp.zeros((bsz, A_HEADS), jnp.float32))
    xs = tuple(jnp.moveaxis(t, 2, 0) for t in (d_c, d_n, m_loc, b_last))
    _, starts = lax.scan(step, init, xs)
    c0, n0, m0 = (jnp.moveaxis(t, 0, 2) for t in starts)
    idx = jnp.arange(CHUNK)
    causal = idx[:, None] >= idx[None, :]
    d_intra = jnp.where(causal, b[..., :, None] - b[..., None, :] + ic[..., None, :], NEG)
    d_inter = b + m0[..., None]
    m_row = jnp.maximum(d_intra.max(-1), d_inter)
    p = jnp.exp(d_intra - m_row[..., None])
    s = jnp.exp(d_inter - m_row)
    qk = jnp.einsum('bhnik,bhnjk->bhnij', qc, kc) * p
    num = jnp.einsum('bhnij,bhnjv->bhniv', qk, vc) + s[..., None] * jnp.einsum('bhnik,bhnkv->bhniv', qc, c0)
    den = qk.sum(-1) + s * jnp.einsum('bhnik,bhnk->bhni', qc, n0)
    h = num / jnp.maximum(jnp.abs(den), jnp.exp(-m_row))[..., None]
    return _merge_chunks(h)[:, pad:]


def _complex_linear_combine(e1, e2):
    a1r, a1i, b1r, b1i = e1
    a2r, a2i, b2r, b2i = e2
    return (a1r * a2r - a1i * a2i, a1r * a2i + a1i * a2r,
            a2r * b1r - a2i * b1i + b2r, a2r * b1i + a2i * b1r + b2i)


def _s5(u, a_re, a_im, log_step, b_re, b_im, c_re, c_im, d_skip, w_glu, b_glu):
    f32 = jnp.float32
    a_re, a_im, log_step, b_re, b_im, c_re, c_im, d_skip = (
        t.astype(f32) for t in (a_re, a_im, log_step, b_re, b_im, c_re, c_im, d_skip))
    bsz, seq_len, _ = u.shape
    ug = u.reshape(bsz, seq_len, B_GROUPS, B_GROUP)
    dt = jnp.exp(log_step)[:, None]
    mag = jnp.exp(a_re * dt)
    lb_re, lb_im = mag * jnp.cos(a_im * dt), mag * jnp.sin(a_im * dt)
    inv = 1.0 / (a_re * a_re + a_im * a_im)
    zr, zi = lb_re - 1.0, lb_im
    fr = (zr * a_re + zi * a_im) * inv
    fi = (zi * a_re - zr * a_im) * inv
    bb_re = fr[..., None] * b_re - fi[..., None] * b_im
    bb_im = fr[..., None] * b_im + fi[..., None] * b_re
    bu_re = jnp.einsum('blgc,gpc->blgp', ug, bb_re)
    bu_im = jnp.einsum('blgc,gpc->blgp', ug, bb_im)
    lam_re = jnp.broadcast_to(lb_re, (1, seq_len) + lb_re.shape)
    lam_im = jnp.broadcast_to(lb_im, (1, seq_len) + lb_im.shape)
    _, _, xr, xi = lax.associative_scan(_complex_linear_combine, (lam_re, lam_im, bu_re, bu_im), axis=1)
    y = (jnp.einsum('blgp,gcp->blgc', xr, c_re) - jnp.einsum('blgp,gcp->blgc', xi, c_im)
         + d_skip * ug)
    y = jax.nn.gelu(y.reshape(bsz, seq_len, B_CH))
    return y * jax.nn.sigmoid(y @ w_glu.astype(f32) + b_glu.astype(f32))


def _even_mixer(x, w_in, gate_bias, head_norm, a_re, a_im, log_step, b_re, b_im, c_re, c_im,
                d_skip, w_glu, b_glu, w_out):
    bsz, seq_len, _ = x.shape
    proj = (x @ w_in).astype(jnp.float32)
    q, k, v, o_pre, if_pre, u = jnp.split(
        proj, [A_Q, 2 * A_Q, 2 * A_Q + A_V, 2 * A_Q + 2 * A_V, 2 * A_Q + 2 * A_V + 2 * A_HEADS], axis=-1)
    q = q.reshape(bsz, seq_len, A_HEADS, A_DQK)
    k = k.reshape(bsz, seq_len, A_HEADS, A_DQK) * (A_DQK ** -0.5)
    v = v.reshape(bsz, seq_len, A_HEADS, A_DV)
    if_pre = A_GATE_CAP * jnp.tanh((if_pre + gate_bias.astype(jnp.float32)) / A_GATE_CAP)
    i_pre, f_pre = if_pre[..., :A_HEADS], if_pre[..., A_HEADS:]
    h = _mlstm_chunkwise(q, k, v, i_pre, jax.nn.log_sigmoid(f_pre))
    mu = h.mean(-1, keepdims=True)
    var = jnp.square(h - mu).mean(-1, keepdims=True)
    h = ((h - mu) * lax.rsqrt(var + 1e-6)).reshape(bsz, seq_len, A_V) * head_norm.astype(jnp.float32)
    h_a = jax.nn.sigmoid(o_pre) * h
    h_b = _s5(u, a_re, a_im, log_step, b_re, b_im, c_re, c_im, d_skip, w_glu, b_glu)
    return jnp.concatenate([h_a, h_b], axis=-1).astype(x.dtype) @ w_out


def _causal_dwconv(x, w):
    kw, ch = w.shape
    return lax.conv_general_dilated(x, w[:, None, :], window_strides=(1,), padding=[(kw - 1, 0)],
                                    dimension_numbers=('NWC', 'WIO', 'NWC'), feature_group_count=ch)


def _gated_delta_chunkwise(q, k, v, g, beta):
    pad = CHUNK - N_META
    q, k, v, g, beta = (_pad_front(t, pad, 0.0) for t in (q, k, v, g, beta))
    nc = q.shape[1] // CHUNK
    qc, kc, vc, gc, bc = (_split_chunks(t, nc) for t in (q, k, v, g, beta))
    gcum = jnp.cumsum(gc, axis=-1)
    idx = jnp.arange(CHUNK)
    incl = idx[:, None] >= idx[None, :]
    strict = idx[:, None] > idx[None, :]
    decay = jnp.exp(jnp.where(incl, gcum[..., :, None] - gcum[..., None, :], NEG))
    kk = jnp.einsum('bhnik,bhnjk->bhnij', kc, kc)
    lmat = jnp.where(strict, bc[..., :, None] * kk * decay, 0.0) + jnp.eye(CHUNK, dtype=jnp.float32)
    rhs = jnp.concatenate([bc[..., None] * vc, (bc * jnp.exp(gcum))[..., None] * kc], axis=-1)
    sol = lax.linalg.triangular_solve(lmat, rhs, left_side=True, lower=True, unit_diagonal=True)
    u_c, w_c = sol[..., :C_DV], sol[..., C_DV:]
    attn = jnp.einsum('bhnik,bhnjk->bhnij', qc, kc) * decay
    q_dec = qc * jnp.exp(gcum)[..., None]
    k_dec = kc * jnp.exp(gcum[..., -1:] - gcum)[..., None]
    g_last = gcum[..., -1]

    def step(s, inp):
        u_i, w_i, a_i, qd_i, kd_i, gl_i = inp
        v_new = u_i - jnp.einsum('bhck,bhkv->bhcv', w_i, s)
        o = jnp.einsum('bhck,bhkv->bhcv', qd_i, s) + jnp.einsum('bhij,bhjv->bhiv', a_i, v_new)
        s = jnp.exp(gl_i)[..., None, None] * s + jnp.einsum('bhck,bhcv->bhkv', kd_i, v_new)
        return s, o

    s0 = jnp.zeros((q.shape[0], C_HEADS, C_DK, C_DV), jnp.float32)
    xs = tuple(jnp.moveaxis(t, 2, 0) for t in (u_c, w_c, attn, q_dec, k_dec, g_last))
    _, o = lax.scan(step, s0, xs)
    return _merge_chunks(jnp.moveaxis(o, 0, 2))[:, pad:]


def _odd_mixer(x, w_in, conv_w, a_log, dt_bias, norm_w, w_out):
    bsz, seq_len, _ = x.shape
    proj = (x @ w_in).astype(jnp.float32)
    qkv, z, b, a = jnp.split(proj, [3 * C_W, 4 * C_W, 4 * C_W + C_HEADS], axis=-1)
    qkv = jax.nn.silu(_causal_dwconv(qkv, conv_w.astype(jnp.float32)))
    q, k, v = jnp.split(qkv, 3, axis=-1)
    q = _l2norm(q.reshape(bsz, seq_len, C_HEADS, C_DK)) * (C_DK ** -0.5)
    k = _l2norm(k.reshape(bsz, seq_len, C_HEADS, C_DK))
    v = v.reshape(bsz, seq_len, C_HEADS, C_DV)
    beta = jax.nn.sigmoid(b)
    g = -jnp.exp(a_log.astype(jnp.float32)) * jax.nn.softplus(a + dt_bias.astype(jnp.float32))
    o = _gated_delta_chunkwise(q, k, v, g, beta)
    o = o * lax.rsqrt(jnp.mean(o * o, axis=-1, keepdims=True) + 1e-6) * norm_w.astype(jnp.float32)
    o = o.reshape(bsz, seq_len, C_W) * jax.nn.silu(z)
    return o.astype(x.dtype) @ w_out


def _moe(x, w_router, b_router, w_gate_up, b_gate_up, w_down, b_down):
    bsz, seq_len, d = x.shape
    xt = x.reshape(-1, d)
    n_tok = xt.shape[0]
    logits = (xt @ w_router + b_router).astype(jnp.float32)
    top_v, top_e = lax.top_k(logits, TOP_K)
    gates = jax.nn.softmax(top_v, axis=-1)
    n_assign = n_tok * TOP_K
    e_flat = top_e.reshape(-1)
    tok_flat = jnp.repeat(jnp.arange(n_tok, dtype=jnp.int32), TOP_K)
    g_flat = gates.reshape(-1)
    order = jnp.argsort(e_flat)
    e_sorted = e_flat[order]
    counts = jnp.bincount(e_flat, length=N_EXPERTS)
    padded = (counts + MOE_BLOCK - 1) // MOE_BLOCK * MOE_BLOCK
    start = jnp.cumsum(counts) - counts
    ends_p = jnp.cumsum(padded)
    pstart = ends_p - padded
    dest = pstart[e_sorted] + (jnp.arange(n_assign) - start[e_sorted])
    n_blocks = -(-n_assign // MOE_BLOCK) + N_EXPERTS
    n_slots = n_blocks * MOE_BLOCK
    slot_tok = jnp.zeros((n_slots,), jnp.int32).at[dest].set(tok_flat[order])
    slot_gate = jnp.zeros((n_slots,), jnp.float32).at[dest].set(g_flat[order])
    block_e = jnp.minimum(jnp.searchsorted(ends_p, jnp.arange(n_blocks) * MOE_BLOCK, side='right'),
                          N_EXPERTS - 1)

    def expert_block(args):
        e, toks = args
        h = xt[toks] @ w_gate_up[e] + b_gate_up[e]
        gate = jnp.minimum(h[:, ::2], SWIGLU_LIMIT)
        up = jnp.clip(h[:, 1::2], -SWIGLU_LIMIT, SWIGLU_LIMIT)
        act = (up + 1.0) * gate * jax.nn.sigmoid(SWIGLU_ALPHA * gate)
        return act @ w_down[e] + b_down[e]

    out = lax.map(expert_block, (block_e, slot_tok.reshape(n_blocks, MOE_BLOCK)))
    out = (out.reshape(n_slots, d) * slot_gate[:, None]).astype(xt.dtype)
    y = jnp.zeros_like(xt).at[slot_tok].add(out)
    return y.reshape(bsz, seq_len, d)


def setup_inputs(seed: int = 0) -> dict:
    key = jax.random.key(seed)
    ks = iter(jax.random.split(key, 48))
    nrm = lambda shape, scale: scale * jax.random.normal(next(ks), shape, jnp.float32)
    uni = lambda shape, lo, hi: jax.random.uniform(next(ks), shape, jnp.float32, lo, hi)
    x = nrm((BATCH, SEQ, D_MODEL), 1.0)
    meta_tokens = nrm((N_META, D_MODEL), 1.0)
    ln_g = 1.0 + nrm((DEPTH, 2, D_MODEL), 0.02)
    ln_b = nrm((DEPTH, 2, D_MODEL), 0.02)
    ev_w_in = nrm((N_EVEN, D_MODEL, EVEN_IN), D_MODEL ** -0.5)
    f_bias = jnp.broadcast_to(jnp.linspace(3.0, 6.0, A_HEADS), (N_EVEN, A_HEADS))
    ev_gate_bias = jnp.concatenate([jnp.zeros((N_EVEN, A_HEADS)), f_bias], axis=-1) + nrm((N_EVEN, 2 * A_HEADS), 0.1)
    ev_head_norm = 1.0 + nrm((N_EVEN, A_V), 0.02)
    s5_a_re = -0.5 + nrm((N_EVEN, B_GROUPS, B_STATE), 0.01)
    s5_a_im = jnp.pi * jnp.arange(B_STATE, dtype=jnp.float32) + nrm((N_EVEN, B_GROUPS, B_STATE), 0.01)
    s5_log_step = uni((N_EVEN, B_GROUPS), math.log(1e-3), math.log(1e-1))
    s5_b_re = nrm((N_EVEN, B_GROUPS, B_STATE, B_GROUP), (2 * B_GROUP) ** -0.5)
    s5_b_im = nrm((N_EVEN, B_GROUPS, B_STATE, B_GROUP), (2 * B_GROUP) ** -0.5)
    s5_c_re = nrm((N_EVEN, B_GROUPS, B_GROUP, B_STATE), (2 * B_STATE) ** -0.5)
    s5_c_im = nrm((N_EVEN, B_GROUPS, B_GROUP, B_STATE), (2 * B_STATE) ** -0.5)
    s5_d = nrm((N_EVEN, B_GROUPS, B_GROUP), 1.0)
    s5_w_glu = nrm((N_EVEN, B_CH, B_CH), B_CH ** -0.5)
    s5_b_glu = nrm((N_EVEN, B_CH), 0.01)
    ev_w_out = nrm((N_EVEN, A_V + B_CH, D_MODEL), DN_BETA * (A_V + B_CH) ** -0.5)
    od_w_in = nrm((N_ODD, D_MODEL, ODD_IN), D_MODEL ** -0.5)
    od_conv = nrm((N_ODD, C_CONV, 3 * C_W), C_CONV ** -0.5)
    od_a_log = jnp.log(uni((N_ODD, C_HEADS), 1.0, 16.0))
    dt = jnp.exp(uni((N_ODD, C_HEADS), math.log(1e-3), math.log(1e-1)))
    od_dt_bias = dt + jnp.log(-jnp.expm1(-dt))
    od_norm = 1.0 + nrm((N_ODD, C_DV), 0.02)
    od_w_out = nrm((N_ODD, C_W, D_MODEL), DN_BETA * C_W ** -0.5)
    moe_w_router = nrm((DEPTH, D_MODEL, N_EXPERTS), D_MODEL ** -0.5)
    moe_b_router = nrm((DEPTH, N_EXPERTS), 0.01)
    moe_w_gate_up = nrm((DEPTH, N_EXPERTS, D_MODEL, 2 * D_EXPERT), D_MODEL ** -0.5)
    moe_b_gate_up = nrm((DEPTH, N_EXPERTS, 2 * D_EXPERT), 0.01)
    moe_w_down = nrm((DEPTH, N_EXPERTS, D_EXPERT, D_MODEL), DN_BETA * D_EXPERT ** -0.5)
    moe_b_down = nrm((DEPTH, N_EXPERTS, D_MODEL), 0.01)
    return {'x': x, 'meta_tokens': meta_tokens, 'ln_g': ln_g, 'ln_b': ln_b,
            'ev_w_in': ev_w_in, 'ev_gate_bias': ev_gate_bias, 'ev_head_norm': ev_head_norm,
            's5_a_re': s5_a_re, 's5_a_im': s5_a_im, 's5_log_step': s5_log_step,
            's5_b_re': s5_b_re, 's5_b_im': s5_b_im, 's5_c_re': s5_c_re, 's5_c_im': s5_c_im,
            's5_d': s5_d, 's5_w_glu': s5_w_glu, 's5_b_glu': s5_b_glu, 'ev_w_out': ev_w_out,
            'od_w_in': od_w_in, 'od_conv': od_conv, 'od_a_log': od_a_log, 'od_dt_bias': od_dt_bias,
            'od_norm': od_norm, 'od_w_out': od_w_out,
            'moe_w_router': moe_w_router, 'moe_b_router': moe_b_router,
            'moe_w_gate_up': moe_w_gate_up, 'moe_b_gate_up': moe_b_gate_up,
            'moe_w_down': moe_w_down, 'moe_b_down': moe_b_down}


def reference(x, meta_tokens, ln_g, ln_b, ev_w_in, ev_gate_bias, ev_head_norm,
              s5_a_re, s5_a_im, s5_log_step, s5_b_re, s5_b_im, s5_c_re, s5_c_im,
              s5_d, s5_w_glu, s5_b_glu, ev_w_out,
              od_w_in, od_conv, od_a_log, od_dt_bias, od_norm, od_w_out,
              moe_w_router, moe_b_router, moe_w_gate_up, moe_b_gate_up, moe_w_down, moe_b_down):
    bsz = x.shape[0]
    meta = jnp.broadcast_to(meta_tokens[None], (bsz, N_META, D_MODEL)).astype(x.dtype)
    h = jnp.concatenate([meta, x], axis=1)
    for layer in range(DEPTH):
        j = layer // 2
        if layer % 2 == 0:
            mix = _even_mixer(h, ev_w_in[j], ev_gate_bias[j], ev_head_norm[j],
                              s5_a_re[j], s5_a_im[j], s5_log_step[j], s5_b_re[j], s5_b_im[j],
                              s5_c_re[j], s5_c_im[j], s5_d[j], s5_w_glu[j], s5_b_glu[j], ev_w_out[j])
        else:
            mix = _odd_mixer(h, od_w_in[j], od_conv[j], od_a_log[j], od_dt_bias[j], od_norm[j], od_w_out[j])
        h = _layer_norm(DN_ALPHA * h + mix, ln_g[layer, 0], ln_b[layer, 0])
        ffn = _moe(h, moe_w_router[layer], moe_b_router[layer], moe_w_gate_up[layer],
                   moe_b_gate_up[layer], moe_w_down[layer], moe_b_down[layer])
        h = _layer_norm(DN_ALPHA * h + ffn, ln_g[layer, 1], ln_b[layer, 1])
    return h[:, N_META:]
```

```python
import functools
import math

import jax
import jax.numpy as jnp
from jax import lax
from jax.experimental import pallas as pl
from jax.experimental.pallas import tpu as pltpu

F32 = jnp.float32
BF16 = jnp.bfloat16

D_MODEL = 1024
DEPTH = 4
N_META = 16
CHUNK = 64
PAD = CHUNK - N_META
NEG = -1e30
LN_EPS = 1e-5
DN_ALPHA = (2.0 * DEPTH) ** 0.25

A_HEADS = 4
A_DQK = D_MODEL // 16
A_DV = D_MODEL // 8
A_GATE_CAP = 15.0
B_CH = D_MODEL // 2
B_GROUP = 16
B_GROUPS = B_CH // B_GROUP
B_STATE = 64
S5_CHUNK = 16
C_HEADS = D_MODEL // 128
C_DK = 128
C_CONV = 4
N_EXPERTS = 32
TOP_K = 4
SWIGLU_LIMIT = 7.0
SWIGLU_ALPHA = 1.702
MOE_BLOCK = 256

LANES = 128
VMEM_LIMIT = 56 * 1024 * 1024


def _cparams(*sem):
    return pltpu.CompilerParams(dimension_semantics=sem, vmem_limit_bytes=VMEM_LIMIT)


def _chunk_tile(lp, target):
    best = CHUNK
    for t in range(CHUNK, target + 1, CHUNK):
        if lp % t == 0:
            best = t
    return best


def _dot(a, b):
    return jnp.dot(a, b, preferred_element_type=F32)


def _dot_nt(a, b):
    return lax.dot_general(a, b, (((1,), (1,)), ((), ())), preferred_element_type=F32)


def _dot_tn(a, b):
    return lax.dot_general(a, b, (((0,), (0,)), ((), ())), preferred_element_type=F32)


def _split3(x):
    hi = x.astype(BF16)
    r1 = x - hi.astype(F32)
    mid = r1.astype(BF16)
    lo = (r1 - mid.astype(F32)).astype(BF16)
    return hi, mid, lo


def _dot01_left(t01, x):
    hi, mid, lo = _split3(x)
    return _dot(t01, hi) + _dot(t01, mid) + _dot(t01, lo)


def _dot01_right(x, t01):
    hi, mid, lo = _split3(x)
    return _dot(hi, t01) + _dot(mid, t01) + _dot(lo, t01)


def _sigmoid(x):
    return 1.0 / (1.0 + jnp.exp(-x))


def _softplus(x):
    return jnp.maximum(x, 0.0) + jnp.log(1.0 + jnp.exp(-jnp.abs(x)))


def _log_sigmoid(x):
    return -_softplus(-x)


def _silu(x):
    return x * _sigmoid(x)


def _gelu_tanh(x):
    c = math.sqrt(2.0 / math.pi)
    return 0.5 * x * (1.0 + jnp.tanh(c * (x + 0.044715 * (x * x * x))))


def _iota(shape, dim):
    return lax.broadcasted_iota(jnp.int32, shape, dim)


def _tri_masks(n):
    r = _iota((n, n), 0)
    c = _iota((n, n), 1)
    return r, c


def _proj_kernel(x_ref, w_ref, wg_ref, wgt_ref, y_ref, gcol_ref, grow_ref):
    xb = x_ref[...].astype(BF16)
    y_ref[...] = _dot(xb, w_ref[...])

    @pl.when(pl.program_id(1) == 0)
    def _():
        gcol_ref[...] = _dot(xb, wg_ref[...])
        grow = _dot_nt(wgt_ref[...], xb)
        for j in range(grow_ref.shape[0]):
            grow_ref[j] = grow[:, j * CHUNK:(j + 1) * CHUNK]


def _in_proj(h, w_main, w_gate, w_gate_t, tm, tn):
    r, d = h.shape
    n = w_main.shape[1]
    cpt = tm // CHUNK
    return pl.pallas_call(
        _proj_kernel,
        grid=(r // tm, n // tn),
        in_specs=[pl.BlockSpec((tm, d), lambda i, j: (i, 0)),
                  pl.BlockSpec((d, tn), lambda i, j: (0, j)),
                  pl.BlockSpec((d, LANES), lambda i, j: (0, 0)),
                  pl.BlockSpec((16, d), lambda i, j: (0, 0))],
        out_specs=[pl.BlockSpec((tm, tn), lambda i, j: (i, j)),
                   pl.BlockSpec((tm, LANES), lambda i, j: (i, 0)),
                   pl.BlockSpec((cpt, 16, CHUNK), lambda i, j: (i, 0, 0))],
        out_shape=[jax.ShapeDtypeStruct((r, n), F32),
                   jax.ShapeDtypeStruct((r, LANES), F32),
                   jax.ShapeDtypeStruct((r // CHUNK, 16, CHUNK), F32)],
        compiler_params=_cparams("parallel", "arbitrary"),
        name="in_proj",
    )(h, w_main, w_gate, w_gate_t)


def _mlstm_kernel(q_ref, k_ref, v_ref, o_ref, gcol_ref, grow_ref, bcol_ref, brow_ref, hn_ref,
                  out_ref, c_scr, m_scr):
    c = pl.program_id(1)

    @pl.when(c == 0)
    def _():
        c_scr[...] = jnp.zeros_like(c_scr)
        m_scr[...] = jnp.zeros_like(m_scr)

    n = CHUNK
    ri, ci = _tri_masks(n)
    causal = ri >= ci
    tri_l = jnp.where(causal, 1.0, 0.0).astype(BF16)
    tri_u = jnp.where(ri <= ci, 1.0, 0.0).astype(BF16)
    first = c == 0
    pad_c = jnp.logical_and(first, _iota((n, 1), 0) < PAD)
    pad_r = jnp.logical_and(first, _iota((1, n), 1) < PAD)

    gc = gcol_ref[...] + bcol_ref[...]
    gc = A_GATE_CAP * jnp.tanh(gc * (1.0 / A_GATE_CAP))
    i_c = jnp.where(pad_c, NEG, gc)
    b_c = _dot01_left(tri_l, jnp.where(pad_c, 0.0, _log_sigmoid(gc)))
    gr = grow_ref[0] + brow_ref[...]
    gr = A_GATE_CAP * jnp.tanh(gr * (1.0 / A_GATE_CAP))
    i_r = jnp.where(pad_r, NEG, gr)
    b_r = _dot01_right(jnp.where(pad_r, 0.0, _log_sigmoid(gr)), tri_u)

    one_col = jnp.where(_iota((n, A_DV), 1) == 0, 1.0, 0.0).astype(BF16)
    for h in range(A_HEADS):
        sl = slice(h * LANES, (h + 1) * LANES)
        q = q_ref[:, sl].astype(BF16)
        kf = k_ref[:, sl] * (A_DQK ** -0.5)
        k = kf.astype(BF16)
        vext = jnp.concatenate([v_ref[:, sl].astype(BF16), one_col], axis=1)
        bi = b_c[:, A_HEADS + h:A_HEADS + h + 1]
        ii = i_c[:, h:h + 1]
        bj = b_r[A_HEADS + h:A_HEADS + h + 1, :]
        ij = i_r[h:h + 1, :]
        b_last = bi[n - 1:n, :]
        a_end = b_last - bi + ii
        m_loc = jnp.max(a_end, axis=0, keepdims=True)
        m0 = m_scr[h][0:1, 0:1]
        d_intra = jnp.where(causal, bi - bj + ij, NEG)
        d_inter = bi + m0
        m_row = jnp.maximum(jnp.max(d_intra, axis=1, keepdims=True), d_inter)
        p = jnp.exp(d_intra - m_row)
        s = jnp.exp(d_inter - m_row)
        qk = _dot_nt(q, k) * p
        cext = c_scr[h]
        num_ext = _dot(qk.astype(BF16), vext) + s * _dot(q, cext.astype(BF16))
        num = num_ext[:, :A_DV]
        den = num_ext[:, A_DV:A_DV + 1]
        hh = num / jnp.maximum(jnp.abs(den), jnp.exp(-m_row))
        mu = jnp.mean(hh, axis=1, keepdims=True)
        hc = hh - mu
        var = jnp.mean(hc * hc, axis=1, keepdims=True)
        hnorm = hc * lax.rsqrt(var + 1e-6) * hn_ref[:, sl]
        out_ref[:, sl] = _sigmoid(o_ref[:, sl]) * hnorm
        w_end = jnp.exp(a_end - m_loc)
        d_ext = _dot_tn((kf * w_end).astype(BF16), vext)
        m_new = jnp.maximum(b_last + m0, m_loc)
        c_scr[h] = jnp.exp(b_last + m0 - m_new) * cext + jnp.exp(m_loc - m_new) * d_ext
        m_scr[h] = jnp.broadcast_to(m_new, (8, LANES))


def _mlstm(y, gcol, grow3, bias_col, bias_row, head_norm, bsz, nc):
    r = y.shape[0]
    w = A_HEADS * LANES
    row = lambda b, c: b * nc + c
    return pl.pallas_call(
        _mlstm_kernel,
        grid=(bsz, nc),
        in_specs=[pl.BlockSpec((CHUNK, w), lambda b, c: (row(b, c), 0)),
                  pl.BlockSpec((CHUNK, w), lambda b, c: (row(b, c), 1)),
                  pl.BlockSpec((CHUNK, w), lambda b, c: (row(b, c), 2)),
                  pl.BlockSpec((CHUNK, w), lambda b, c: (row(b, c), 3)),
                  pl.BlockSpec((CHUNK, LANES), lambda b, c: (row(b, c), 0)),
                  pl.BlockSpec((1, 16, CHUNK), lambda b, c: (row(b, c), 0, 0)),
                  pl.BlockSpec((1, LANES), lambda b, c: (0, 0)),
                  pl.BlockSpec((16, CHUNK), lambda b, c: (0, 0)),
                  pl.BlockSpec((1, w), lambda b, c: (0, 0))],
        out_specs=pl.BlockSpec((CHUNK, w), lambda b, c: (row(b, c), 0)),
        out_shape=jax.ShapeDtypeStruct((r, w), F32),
        scratch_shapes=[pltpu.VMEM((A_HEADS, LANES, 2 * LANES), F32),
                        pltpu.VMEM((A_HEADS, 8, LANES), F32)],
        compiler_params=_cparams("parallel", "arbitrary"),
        name="mlstm",
    )(y, y, y, y, gcol, grow3, bias_col, bias_row, head_norm)


def _s5_params(a_re, a_im, log_step, b_re, b_im, c_re, c_im, d_skip):
    hp = lax.Precision.HIGHEST
    g, p = a_re.shape
    dt = jnp.exp(log_step)[:, None]
    mag = jnp.exp(a_re * dt)
    lb_re, lb_im = mag * jnp.cos(a_im * dt), mag * jnp.sin(a_im * dt)
    inv = 1.0 / (a_re * a_re + a_im * a_im)
    zr, zi = lb_re - 1.0, lb_im
    fr = (zr * a_re + zi * a_im) * inv
    fi = (zi * a_re - zr * a_im) * inv
    bb_re = fr[..., None] * b_re - fi[..., None] * b_im
    bb_im = fr[..., None] * b_im + fi[..., None] * b_re
    n = S5_CHUNK
    tau = jnp.arange(n + 1, dtype=F32)[:, None, None]
    pm = jnp.exp(tau * (a_re * dt))
    pr, pi = pm * jnp.cos(tau * (a_im * dt)), pm * jnp.sin(tau * (a_im * dt))
    e_re = pr[..., None] * bb_re - pi[..., None] * bb_im
    e_im = pr[..., None] * bb_im + pi[..., None] * bb_re
    kern = (jnp.einsum('gcp,tgpd->tgcd', c_re, e_re[:n], precision=hp)
            - jnp.einsum('gcp,tgpd->tgcd', c_im, e_im[:n], precision=hp))
    idx = jnp.arange(n)
    diff = idx[None, :] - idx[:, None]
    kd = jnp.where((diff >= 0)[:, :, None, None, None], kern[jnp.clip(diff, 0, n - 1)], 0.0)
    m_t = kd.transpose(2, 0, 4, 1, 3).reshape(g, n * B_GROUP, n * B_GROUP)
    w_re = e_re[n - 1 - idx].transpose(1, 0, 3, 2)
    w_im = e_im[n - 1 - idx].transpose(1, 0, 3, 2)
    w_t = jnp.concatenate([w_re, w_im], axis=-1).reshape(g, n * B_GROUP, 2 * p)
    f_re = c_re[None] * pr[1:, :, None, :] - c_im[None] * pi[1:, :, None, :]
    f_im = c_re[None] * pi[1:, :, None, :] + c_im[None] * pr[1:, :, None, :]
    v_t = jnp.concatenate([f_re.transpose(1, 3, 0, 2), -f_im.transpose(1, 3, 0, 2)], axis=1)
    v_t = v_t.reshape(g, 2 * p, n * B_GROUP)
    lam_a = jnp.concatenate([pr[n], pr[n]], axis=-1)[:, None, :]
    lam_b = jnp.concatenate([-pi[n], pi[n]], axis=-1)[:, None, :]
    d_flat = jnp.tile(d_skip, (1, n))[:, None, :]
    return m_t.astype(BF16), w_t.astype(BF16), v_t.astype(BF16), lam_a, lam_b, d_flat


def _s5_in_kernel(u_ref, mt_ref, wt_ref, d_ref, yi_ref, z_ref):
    u = u_ref[0]
    ub = u.astype(BF16)
    yi_ref[0] = _dot(ub, mt_ref[0]) + d_ref[0] * u
    z_ref[0] = _dot(ub, wt_ref[0])


def _s5_scan_kernel(z_ref, la_ref, lb_ref, x0_ref, x_scr, *, per, bsz):
    @pl.when(pl.program_id(0) == 0)
    def _():
        x_scr[...] = jnp.zeros_like(x_scr)

    la = la_ref[...]
    lb = lb_ref[...]
    x = x_scr[...]
    g, _, st = x.shape
    for i in range(per):
        rows = slice(i * bsz, (i + 1) * bsz)
        x0_ref[:, rows, :] = x
        swapped = pltpu.roll(x.reshape(g * bsz, st), B_STATE, 1).reshape(g, bsz, st)
        x = la * x + lb * swapped + z_ref[:, rows, :]
    x_scr[...] = x


def _s5_out_kernel(yi_ref, x0_ref, vt_ref, y_ref):
    y_ref[0] = yi_ref[0] + _dot(x0_ref[0].astype(BF16), vt_ref[0])


def _s5(u3, params, bsz):
    m_t, w_t, v_t, lam_a, lam_b, d_flat = params
    g, rc, wide = u3.shape
    st = 2 * B_STATE
    gspec = lambda shape: pl.BlockSpec((1,) + shape, lambda i: (i, 0, 0))
    yi, z = pl.pallas_call(
        _s5_in_kernel,
        grid=(g,),
        in_specs=[gspec((rc, wide)), gspec((wide, wide)), gspec((wide, st)), gspec((1, wide))],
        out_specs=[gspec((rc, wide)), gspec((rc, st))],
        out_shape=[jax.ShapeDtypeStruct((g, rc, wide), F32), jax.ShapeDtypeStruct((g, rc, st), F32)],
        compiler_params=_cparams("parallel"),
        name="s5_in",
    )(u3, m_t, w_t, d_flat)
    n_steps = rc // bsz
    per = max(p for p in range(1, 17) if n_steps % p == 0)
    x0 = pl.pallas_call(
        functools.partial(_s5_scan_kernel, per=per, bsz=bsz),
        grid=(n_steps // per,),
        in_specs=[pl.BlockSpec((g, per * bsz, st), lambda i: (0, i, 0)),
                  pl.BlockSpec((g, 1, st), lambda i: (0, 0, 0)),
                  pl.BlockSpec((g, 1, st), lambda i: (0, 0, 0))],
        out_specs=pl.BlockSpec((g, per * bsz, st), lambda i: (0, i, 0)),
        out_shape=jax.ShapeDtypeStruct((g, rc, st), F32),
        scratch_shapes=[pltpu.VMEM((g, bsz, st), F32)],
        compiler_params=_cparams("arbitrary"),
        name="s5_scan",
    )(z, lam_a, lam_b)
    return pl.pallas_call(
        _s5_out_kernel,
        grid=(g,),
        in_specs=[gspec((rc, wide)), gspec((rc, st)), gspec((st, wide))],
        out_specs=gspec((rc, wide)),
        out_shape=jax.ShapeDtypeStruct((g, rc, wide), F32),
        compiler_params=_cparams("parallel"),
        name="s5_out",
    )(yi, x0, v_t)


def _res_ln(h, mix, g, b, pad_rows):
    z = DN_ALPHA * h + mix
    mu = jnp.mean(z, axis=1, keepdims=True)
    zc = z - mu
    var = jnp.mean(zc * zc, axis=1, keepdims=True)
    out = zc * lax.rsqrt(var + LN_EPS) * g + b
    return jnp.where(pad_rows, 0.0, out)


def _pad_rows_mask(tm, lp):
    start = (pl.program_id(0) * tm) % lp
    pos = start + _iota((tm, 1), 0)
    pos = jnp.where(pos >= lp, pos - lp, pos)
    return pos < PAD


def _even_out_kernel(ha_ref, ys_ref, h_ref, wglu_ref, bglu_ref, wout_ref, g_ref, b_ref, out_ref,
                     *, lp):
    tm = h_ref.shape[0]
    yb = _gelu_tanh(ys_ref[...])
    hb = yb * _sigmoid(_dot(yb.astype(BF16), wglu_ref[...]) + bglu_ref[...])
    av = ha_ref.shape[1]
    mix = _dot(ha_ref[...].astype(BF16), wout_ref[:av, :]) + _dot(hb.astype(BF16), wout_ref[av:, :])
    out_ref[...] = _res_ln(h_ref[...], mix, g_ref[...], b_ref[...], _pad_rows_mask(tm, lp))


def _even_out(ha, ys, h, w_glu, b_glu, w_out, ln_g, ln_b, tm, lp):
    r, d = h.shape
    av, bc = ha.shape[1], ys.shape[1]
    full = lambda shape: pl.BlockSpec(shape, lambda i: (0, 0))
    return pl.pallas_call(
        functools.partial(_even_out_kernel, lp=lp),
        grid=(r // tm,),
        in_specs=[pl.BlockSpec((tm, av), lambda i: (i, 0)),
                  pl.BlockSpec((tm, bc), lambda i: (i, 0)),
                  pl.BlockSpec((tm, d), lambda i: (i, 0)),
                  full((bc, bc)), full((1, bc)), full((av + bc, d)), full((1, d)), full((1, d))],
        out_specs=pl.BlockSpec((tm, d), lambda i: (i, 0)),
        out_shape=jax.ShapeDtypeStruct((r, d), F32),
        compiler_params=_cparams("parallel"),
        name="even_out",
    )(ha, ys, h, w_glu, b_glu, w_out, ln_g, ln_b)


def _gdn_kernel(q_ref, k_ref, v_ref, z_ref, gcol_ref, grow_ref, conv_ref, pcol_ref, prow_ref,
                nw_ref, out_ref, s_scr, carry_scr, ext_scr):
    c = pl.program_id(1)

    @pl.when(c == 0)
    def _():
        s_scr[...] = jnp.zeros_like(s_scr)
        carry_scr[...] = jnp.zeros_like(carry_scr)

    n = CHUNK
    ri, ci = _tri_masks(n)
    incl = ri >= ci
    strict = ri > ci
    tri_l = jnp.where(incl, 1.0, 0.0).astype(BF16)
    tri_u = jnp.where(ri <= ci, 1.0, 0.0).astype(BF16)
    first = c == 0
    pad_c = jnp.logical_and(first, _iota((n, 1), 0) < PAD)
    pad_r = jnp.logical_and(first, _iota((1, n), 1) < PAD)

    gc = gcol_ref[...]
    beta_c = jnp.where(pad_c, 0.0, _sigmoid(gc))
    g_c = jnp.where(pad_c, 0.0, pcol_ref[1:2, :] * _softplus(gc + pcol_ref[0:1, :]))
    gcum_c = _dot01_left(tri_l, g_c)
    gr = grow_ref[0]
    g_r = jnp.where(pad_r, 0.0, prow_ref[1] * _softplus(gr + prow_ref[0]))
    gcum_r = _dot01_right(g_r, tri_u)

    width = q_ref.shape[1]
    ext_scr[0:8, :] = carry_scr[...]
    for j, ref in enumerate((q_ref, k_ref, v_ref)):
        ext_scr[8:8 + n, j * width:(j + 1) * width] = ref[...]
    carry_scr[...] = ext_scr[n:n + 8, :]

    def conv(col):
        acc = conv_ref[0:1, col] * ext_scr[5:5 + n, col]
        for j in range(1, C_CONV):
            acc = acc + conv_ref[j:j + 1, col] * ext_scr[5 + j:5 + j + n, col]
        return _silu(acc)

    def l2n(x):
        return x * lax.rsqrt(jnp.sum(x * x, axis=1, keepdims=True) + 1e-6)

    for h in range(C_HEADS):
        sl = slice(h * LANES, (h + 1) * LANES)
        q = l2n(conv(slice(h * LANES, (h + 1) * LANES))) * (C_DK ** -0.5)
        k = l2n(conv(slice(width + h * LANES, width + (h + 1) * LANES)))
        v = conv(slice(2 * width + h * LANES, 2 * width + (h + 1) * LANES))
        beta = beta_c[:, h:h + 1]
        gi = gcum_c[:, C_HEADS + h:C_HEADS + h + 1]
        gj = gcum_r[C_HEADS + h:C_HEADS + h + 1, :]
        g_last = gi[n - 1:n, :]
        decay = jnp.where(incl, jnp.exp(jnp.where(incl, gi - gj, 0.0)), 0.0)
        kb = k.astype(BF16)
        qkk = _dot_nt(jnp.concatenate([q.astype(BF16), kb], axis=0), kb)
        attn = qkk[:n] * decay
        a = jnp.where(strict, -(beta * qkk[n:] * decay), 0.0)
        eg = jnp.exp(gi)
        x = jnp.concatenate([beta * v, (beta * eg) * k], axis=1)
        ab = a.astype(BF16)
        x = x + _dot(ab, x.astype(BF16))
        for _ in range(5):
            a = _dot(ab, ab)
            ab = a.astype(BF16)
            x = x + _dot(ab, x.astype(BF16))
        u_c = x[:, :LANES]
        w_c = x[:, LANES:]
        s0 = s_scr[h]
        s0b = s0.astype(BF16)
        q_dec = (q * eg).astype(BF16)
        k_dec = (k * jnp.exp(g_last - gi)).astype(BF16)
        ws_qs = _dot(jnp.concatenate([w_c.astype(BF16), q_dec], axis=0), s0b)
        v_new = u_c - ws_qs[:n]
        vb = v_new.astype(BF16)
        o = ws_qs[n:] + _dot(attn.astype(BF16), vb)
        s_scr[h] = jnp.exp(g_last) * s0 + _dot_tn(k_dec, vb)
        o = o * lax.rsqrt(jnp.mean(o * o, axis=1, keepdims=True) + 1e-6) * nw_ref[...]
        out_ref[:, sl] = o * _silu(z_ref[:, sl])


def _gdn(y, gcol, grow3, conv_w, pcol, prow, norm_w, bsz, nc):
    r = y.shape[0]
    w = C_HEADS * C_DK
    row = lambda b, c: b * nc + c
    return pl.pallas_call(
        _gdn_kernel,
        grid=(bsz, nc),
        in_specs=[pl.BlockSpec((CHUNK, w), lambda b, c: (row(b, c), 0)),
                  pl.BlockSpec((CHUNK, w), lambda b, c: (row(b, c), 1)),
                  pl.BlockSpec((CHUNK, w), lambda b, c: (row(b, c), 2)),
                  pl.BlockSpec((CHUNK, w), lambda b, c: (row(b, c), 3)),
                  pl.BlockSpec((CHUNK, LANES), lambda b, c: (row(b, c), 0)),
                  pl.BlockSpec((1, 16, CHUNK), lambda b, c: (row(b, c), 0, 0)),
                  pl.BlockSpec((C_CONV, 3 * w), lambda b, c: (0, 0)),
                  pl.BlockSpec((2, LANES), lambda b, c: (0, 0)),
                  pl.BlockSpec((2, 16, CHUNK), lambda b, c: (0, 0, 0)),
                  pl.BlockSpec((1, LANES), lambda b, c: (0, 0))],
        out_specs=pl.BlockSpec((CHUNK, w), lambda b, c: (row(b, c), 0)),
        out_shape=jax.ShapeDtypeStruct((r, w), F32),
        scratch_shapes=[pltpu.VMEM((C_HEADS, C_DK, C_DK), F32),
                        pltpu.VMEM((8, 3 * w), F32),
                        pltpu.VMEM((CHUNK + 8, 3 * w), F32)],
        compiler_params=_cparams("parallel", "arbitrary"),
        name="gdn",
    )(y, y, y, y, gcol, grow3, conv_w, pcol, prow, norm_w)


def _odd_out_kernel(o_ref, h_ref, wout_ref, g_ref, b_ref, out_ref, *, lp):
    tm = h_ref.shape[0]
    mix = _dot(o_ref[...].astype(BF16), wout_ref[...])
    out_ref[...] = _res_ln(h_ref[...], mix, g_ref[...], b_ref[...], _pad_rows_mask(tm, lp))


def _odd_out(o, h, w_out, ln_g, ln_b, tm, lp):
    r, d = h.shape
    full = lambda shape: pl.BlockSpec(shape, lambda i: (0, 0))
    return pl.pallas_call(
        functools.partial(_odd_out_kernel, lp=lp),
        grid=(r // tm,),
        in_specs=[pl.BlockSpec((tm, d), lambda i: (i, 0)),
                  pl.BlockSpec((tm, d), lambda i: (i, 0)),
                  full((d, d)), full((1, d)), full((1, d))],
        out_specs=pl.BlockSpec((tm, d), lambda i: (i, 0)),
        out_shape=jax.ShapeDtypeStruct((r, d), F32),
        compiler_params=_cparams("parallel"),
        name="odd_out",
    )(o, h, w_out, ln_g, ln_b)


def _router_kernel(x_ref, w_ref, b_ref, meta_ref, gate_ref, cnt_ref, base_scr):
    @pl.when(pl.program_id(0) == 0)
    def _():
        base_scr[...] = jnp.zeros_like(base_scr)

    tm = x_ref.shape[0]
    logits = _dot(x_ref[...].astype(BF16), w_ref[...]) + b_ref[...]
    lane = _iota((tm, LANES), 1)
    lane_f = lane.astype(F32)
    work = logits
    vals, sels = [], []
    onehot = jnp.zeros((tm, LANES), F32)
    for _ in range(TOP_K):
        m = jnp.max(work, axis=1, keepdims=True)
        idx = jnp.min(jnp.where(work == m, lane_f, float(LANES)), axis=1, keepdims=True)
        sel = lane_f == idx
        vals.append(m)
        sels.append((sel, idx))
        onehot = onehot + jnp.where(sel, 1.0, 0.0)
        work = jnp.where(sel, -jnp.inf, work)
    ri, ci = _tri_masks(tm)
    tri = jnp.where(ri > ci, 1.0, 0.0).astype(BF16)
    before = _dot(tri, onehot.astype(BF16)) + base_scr[0:1, :]
    base_scr[...] = base_scr[...] + jnp.sum(onehot, axis=0, keepdims=True)
    cnt_ref[...] = base_scr[...]
    es = [jnp.exp(v - vals[0]) for v in vals]
    tot = es[0] + es[1] + es[2] + es[3]
    meta = jnp.zeros((tm, LANES), jnp.int32)
    gate = jnp.zeros((tm, LANES), F32)
    for k, (sel, idx) in enumerate(sels):
        rank = jnp.sum(jnp.where(sel, before, 0.0), axis=1, keepdims=True)
        meta = jnp.where(lane == k, idx.astype(jnp.int32), meta)
        meta = jnp.where(lane == TOP_K + k, rank.astype(jnp.int32), meta)
        gate = jnp.where(lane == k, es[k] / tot, gate)
    meta_ref[...] = meta
    gate_ref[...] = gate


def _router(h, w_r, b_r, tm):
    r, d = h.shape
    return pl.pallas_call(
        _router_kernel,
        grid=(r // tm,),
        in_specs=[pl.BlockSpec((tm, d), lambda i: (i, 0)),
                  pl.BlockSpec((d, LANES), lambda i: (0, 0)),
                  pl.BlockSpec((1, LANES), lambda i: (0, 0))],
        out_specs=[pl.BlockSpec((tm, LANES), lambda i: (i, 0)),
                   pl.BlockSpec((tm, LANES), lambda i: (i, 0)),
                   pl.BlockSpec((8, LANES), lambda i: (0, 0))],
        out_shape=[jax.ShapeDtypeStruct((r, LANES), jnp.int32),
                   jax.ShapeDtypeStruct((r, LANES), F32),
                   jax.ShapeDtypeStruct((8, LANES), F32)],
        scratch_shapes=[pltpu.VMEM((8, LANES), F32)],
        compiler_params=_cparams("arbitrary"),
        name="router",
    )(h, w_r, b_r)


def _dispatch_kernel(dest_ref, x_hbm, xs_in, xs_hbm, sem, *, tm):
    del xs_in
    base = pl.program_id(0) * tm
    n = tm * TOP_K

    def copy(src_row, dst_row):
        return pltpu.make_async_copy(x_hbm.at[pl.ds(src_row, 1)], xs_hbm.at[pl.ds(dst_row, 1)], sem)

    def issue(t, carry):
        for k in range(TOP_K):
            copy(base + t, dest_ref[t * TOP_K + k]).start()
        return carry

    lax.fori_loop(0, tm, issue, 0)

    def drain(j, carry):
        copy(0, 0).wait()
        return carry

    lax.fori_loop(0, n, drain, 0)


def _dispatch(h, dest_flat, n_slots, tm):
    r, d = h.shape
    xs0 = jnp.zeros((n_slots, d), h.dtype)
    return pl.pallas_call(
        functools.partial(_dispatch_kernel, tm=tm),
        grid=(r // tm,),
        in_specs=[pl.BlockSpec((tm * TOP_K,), lambda i: (i,), memory_space=pltpu.SMEM),
                  pl.BlockSpec(memory_space=pl.ANY),
                  pl.BlockSpec(memory_space=pl.ANY)],
        out_specs=pl.BlockSpec(memory_space=pl.ANY),
        out_shape=jax.ShapeDtypeStruct((n_slots, d), h.dtype),
        scratch_shapes=[pltpu.SemaphoreType.DMA],
        input_output_aliases={2: 0},
        compiler_params=_cparams("arbitrary"),
        name="moe_dispatch",
    )(dest_flat, h, xs0)


def _expert_kernel(be_ref, nu_ref, x_ref, wgu_ref, bgu_ref, wd_ref, bd_ref, out_ref, wgu_scr, wd_scr):
    b = pl.program_id(0)
    prev = be_ref[jnp.maximum(b - 1, 0)]
    fresh = jnp.logical_or(b == 0, be_ref[b] != prev)
    tile = 2 * LANES
    n_tiles = wgu_ref.shape[2] // tile

    @pl.when(fresh)
    def _():
        r = _iota((tile, tile), 0)
        c = _iota((tile, tile), 1)
        src = jnp.where(c < LANES, 2 * c, 2 * (c - LANES) + 1)
        perm = jnp.where(r == src, 1.0, 0.0).astype(BF16)
        for j in range(n_tiles):
            cols = slice(j * tile, (j + 1) * tile)
            wgu_scr[:, cols] = _dot(wgu_ref[0, :, cols].astype(BF16), perm).astype(BF16)
        wd_scr[...] = wd_ref[0].astype(BF16)

    @pl.when(b < nu_ref[0])
    def _():
        xb = x_ref[...].astype(BF16)
        h = _dot(xb, wgu_scr[...]) + bgu_ref[0]
        acts = []
        for j in range(n_tiles):
            gate = jnp.minimum(h[:, j * tile:j * tile + LANES], SWIGLU_LIMIT)
            up = jnp.clip(h[:, j * tile + LANES:(j + 1) * tile], -SWIGLU_LIMIT, SWIGLU_LIMIT)
            acts.append(((up + 1.0) * gate * _sigmoid(SWIGLU_ALPHA * gate)).astype(BF16))
        act = jnp.concatenate(acts, axis=1)
        out_ref[...] = _dot(act, wd_scr[...]) + bd_ref[0]

    @pl.when(b >= nu_ref[0])
    def _():
        out_ref[...] = jnp.zeros_like(out_ref)


def _experts(xs, block_e, n_used, w_gu, b_gu, w_d, b_d):
    n_slots, d = xs.shape
    n_blocks = n_slots // MOE_BLOCK
    de2 = w_gu.shape[2]
    de = w_d.shape[1]
    grid_spec = pltpu.PrefetchScalarGridSpec(
        num_scalar_prefetch=2,
        grid=(n_blocks,),
        in_specs=[pl.BlockSpec((MOE_BLOCK, d), lambda b, be, nu: (b, 0)),
                  pl.BlockSpec((1, d, de2), lambda b, be, nu: (be[b], 0, 0)),
                  pl.BlockSpec((1, 1, de2), lambda b, be, nu: (be[b], 0, 0)),
                  pl.BlockSpec((1, de, d), lambda b, be, nu: (be[b], 0, 0)),
                  pl.BlockSpec((1, 1, d), lambda b, be, nu: (be[b], 0, 0))],
        out_specs=pl.BlockSpec((MOE_BLOCK, d), lambda b, be, nu: (b, 0)),
        scratch_shapes=[pltpu.VMEM((d, de2), BF16), pltpu.VMEM((de, d), BF16)],
    )
    return pl.pallas_call(
        _expert_kernel,
        grid_spec=grid_spec,
        out_shape=jax.ShapeDtypeStruct((n_slots, d), F32),
        compiler_params=_cparams("arbitrary"),
        name="moe_experts",
    )(block_e, n_used, xs, w_gu, b_gu, w_d, b_d)


def _combine_kernel(dest_ref, gate_ref, h_ref, g_ref, b_ref, eo_hbm, out_ref, buf, sem,
                    *, lp):
    tm = h_ref.shape[0]
    n = tm * TOP_K

    def copy(src_row, k, t):
        return pltpu.make_async_copy(eo_hbm.at[pl.ds(src_row, 1)], buf.at[k, pl.ds(t, 1)], sem)

    def issue(t, carry):
        for k in range(TOP_K):
            copy(dest_ref[t * TOP_K + k], k, t).start()
        return carry

    lax.fori_loop(0, tm, issue, 0)

    def drain(j, carry):
        copy(0, 0, 0).wait()
        return carry

    lax.fori_loop(0, n, drain, 0)

    gates = gate_ref[...]
    ffn = gates[:, 0:1] * buf[0]
    for k in range(1, TOP_K):
        ffn = ffn + gates[:, k:k + 1] * buf[k]
    out_ref[...] = _res_ln(h_ref[...], ffn, g_ref[...], b_ref[...], _pad_rows_mask(tm, lp))


def _combine(eo, dest_flat, gates, h, ln_g, ln_b, tm, lp):
    r, d = h.shape
    full = lambda shape: pl.BlockSpec(shape, lambda i: (0, 0))
    return pl.pallas_call(
        functools.partial(_combine_kernel, lp=lp),
        grid=(r // tm,),
        in_specs=[pl.BlockSpec((tm * TOP_K,), lambda i: (i,), memory_space=pltpu.SMEM),
                  pl.BlockSpec((tm, LANES), lambda i: (i, 0)),
                  pl.BlockSpec((tm, d), lambda i: (i, 0)),
                  full((1, d)), full((1, d)),
                  pl.BlockSpec(memory_space=pl.ANY)],
        out_specs=pl.BlockSpec((tm, d), lambda i: (i, 0)),
        out_shape=jax.ShapeDtypeStruct((r, d), F32),
        scratch_shapes=[pltpu.VMEM((TOP_K, tm, d), F32), pltpu.SemaphoreType.DMA],
        compiler_params=_cparams("arbitrary"),
        name="moe_combine",
    )(dest_flat, gates, h, ln_g, ln_b, eo)


def _moe(h, w_router, b_router, w_gu, b_gu, w_d, b_d, ln_g, ln_b, tm, lp):
    r, d = h.shape
    w_r = jnp.pad(w_router, ((0, 0), (0, LANES - N_EXPERTS))).astype(BF16)
    b_r = jnp.pad(b_router, (0, LANES - N_EXPERTS), constant_values=NEG)[None, :]
    meta, gates, cnt = _router(h, w_r, b_r, tm)
    counts = cnt[0, :N_EXPERTS].astype(jnp.int32)
    padded = (counts + MOE_BLOCK - 1) // MOE_BLOCK * MOE_BLOCK
    ends_p = jnp.cumsum(padded)
    pstart = ends_p - padded
    n_blocks = -(-(r * TOP_K) // MOE_BLOCK) + N_EXPERTS
    n_slots = n_blocks * MOE_BLOCK
    block_e = jnp.minimum(jnp.searchsorted(ends_p, jnp.arange(n_blocks, dtype=jnp.int32) * MOE_BLOCK,
                                           side='right'), N_EXPERTS - 1).astype(jnp.int32)
    n_used = (ends_p[-1:] // MOE_BLOCK).astype(jnp.int32)
    dest = (pstart[meta[:, :TOP_K]] + meta[:, TOP_K:2 * TOP_K]).reshape(-1)
    xs = _dispatch(h, dest, n_slots, tm)
    half = w_gu.shape[2] // 2
    b_gu_t = b_gu.reshape(N_EXPERTS, half // LANES, LANES, 2).transpose(0, 1, 3, 2)
    b_gu_t = b_gu_t.reshape(N_EXPERTS, 1, 2 * half)
    eo = _experts(xs, block_e, n_used, w_gu, b_gu_t, w_d, b_d[:, None, :])
    return _combine(eo, dest, gates, h, ln_g, ln_b, tm, lp)


def _pad_heads(w, heads, dim):
    k = w.shape[0]
    return jnp.pad(w.reshape(k, heads, dim), ((0, 0), (0, 0), (0, LANES - dim))).reshape(k, heads * LANES)


def _gate_weights(wg):
    k, n = wg.shape
    return (jnp.pad(wg, ((0, 0), (0, LANES - n))).astype(BF16),
            jnp.pad(wg.T, ((0, 16 - n), (0, 0))).astype(BF16))


def _even_layer(h, p, bsz, nc, tm, lp):
    r = h.shape[0]
    w_in = p['w_in']
    aq, av = A_HEADS * A_DQK, A_HEADS * A_DV
    o0 = 2 * aq + 2 * av
    w_main = jnp.concatenate([_pad_heads(w_in[:, :aq], A_HEADS, A_DQK),
                              _pad_heads(w_in[:, aq:2 * aq], A_HEADS, A_DQK),
                              w_in[:, 2 * aq:2 * aq + av], w_in[:, 2 * aq + av:o0],
                              w_in[:, o0 + 2 * A_HEADS:]], axis=1).astype(BF16)
    w_gate, w_gate_t = _gate_weights(w_in[:, o0:o0 + 2 * A_HEADS])
    y, gcol, grow = _in_proj(h, w_main, w_gate, w_gate_t, _chunk_tile(r // bsz, 704), w_main.shape[1])
    gb = p['gate_bias']
    bias_col = jnp.pad(gb, (0, LANES - gb.shape[0]))[None, :]
    bias_row = jnp.broadcast_to(jnp.pad(gb, (0, 16 - gb.shape[0]))[:, None], (16, CHUNK))
    ha = _mlstm(y, gcol, grow, bias_col, bias_row, p['head_norm'][None, :], bsz, nc)
    nch = r // bsz // S5_CHUNK
    u = y[:, 4 * A_HEADS * LANES:]
    u3 = u.reshape(bsz, nch, S5_CHUNK, B_GROUPS, B_GROUP).transpose(3, 1, 0, 2, 4)
    u3 = u3.reshape(B_GROUPS, nch * bsz, S5_CHUNK * B_GROUP)
    y3 = _s5(u3, _s5_params(p['a_re'], p['a_im'], p['log_step'], p['b_re'], p['b_im'],
                            p['c_re'], p['c_im'], p['d']), bsz)
    ys = y3.reshape(B_GROUPS, nch, bsz, S5_CHUNK, B_GROUP).transpose(2, 1, 3, 0, 4).reshape(r, B_CH)
    return _even_out(ha, ys, h, p['w_glu'].astype(BF16), p['b_glu'][None, :], p['w_out'].astype(BF16),
                     p['ln_g'][None, :], p['ln_b'][None, :], tm, lp)


def _odd_layer(h, p, bsz, nc, tm, lp):
    w_in = p['w_in']
    cw = C_HEADS * C_DK
    w_main = w_in[:, :4 * cw].astype(BF16)
    w_gate, w_gate_t = _gate_weights(w_in[:, 4 * cw:])
    y, gcol, grow = _in_proj(h, w_main, w_gate, w_gate_t, _chunk_tile(h.shape[0] // bsz, 704),
                             w_main.shape[1] // 2)
    neg_a = -jnp.exp(p['a_log'])
    zeros = jnp.zeros((C_HEADS,), F32)
    dt16 = jnp.concatenate([zeros, p['dt_bias']])
    na16 = jnp.concatenate([zeros, neg_a])
    pcol = jnp.pad(jnp.stack([dt16, na16]), ((0, 0), (0, LANES - 16)))
    prow = jnp.broadcast_to(jnp.stack([dt16, na16])[:, :, None], (2, 16, CHUNK))
    o = _gdn(y, gcol, grow, p['conv'], pcol, prow, p['norm'][None, :], bsz, nc)
    return _odd_out(o, h, p['w_out'].astype(BF16), p['ln_g'][None, :], p['ln_b'][None, :], tm, lp)


def kernel(x, meta_tokens, ln_g, ln_b, ev_w_in, ev_gate_bias, ev_head_norm, s5_a_re, s5_a_im, s5_log_step, s5_b_re, s5_b_im, s5_c_re, s5_c_im, s5_d, s5_w_glu, s5_b_glu, ev_w_out, od_w_in, od_conv, od_a_log, od_dt_bias, od_norm, od_w_out, moe_w_router, moe_b_router, moe_w_gate_up, moe_b_gate_up, moe_w_down, moe_b_down):
    bsz, seq, d = x.shape
    lp = PAD + N_META + seq
    assert lp % CHUNK == 0 and d == D_MODEL
    nc = lp // CHUNK
    r = bsz * lp
    assert r % MOE_BLOCK == 0 and lp >= MOE_BLOCK
    tm = 512 if (r % 512 == 0 and lp >= 512) else MOE_BLOCK
    meta =jnp.broadcast_to(meta_tokens[None], (bsz, N_META, d)).astype(x.dtype)
    h = jnp.concatenate([jnp.zeros((bsz, PAD, d), x.dtype), meta, x], axis=1).reshape(bsz * lp, d)
    for layer in range(ln_g.shape[0]):
        j = layer // 2
        if layer % 2 == 0:
            p = dict(w_in=ev_w_in[j], gate_bias=ev_gate_bias[j], head_norm=ev_head_norm[j],
                     a_re=s5_a_re[j], a_im=s5_a_im[j], log_step=s5_log_step[j], b_re=s5_b_re[j],
                     b_im=s5_b_im[j], c_re=s5_c_re[j], c_im=s5_c_im[j], d=s5_d[j],
                     w_glu=s5_w_glu[j], b_glu=s5_b_glu[j], w_out=ev_w_out[j],
                     ln_g=ln_g[layer, 0], ln_b=ln_b[layer, 0])
            h = _even_layer(h, p, bsz, nc, tm, lp)
        else:
            p = dict(w_in=od_w_in[j], conv=od_conv[j], a_log=od_a_log[j], dt_bias=od_dt_bias[j],
                     norm=od_norm[j], w_out=od_w_out[j], ln_g=ln_g[layer, 0], ln_b=ln_b[layer, 0])
            h = _odd_layer(h, p, bsz, nc, tm, lp)
        h = _moe(h, moe_w_router[layer], moe_b_router[layer], moe_w_gate_up[layer],
                 moe_b_gate_up[layer], moe_w_down[layer], moe_b_down[layer],
                 ln_g[layer, 1][None, :], ln_b[layer, 1][None, :], MOE_BLOCK, lp)
    return h.reshape(bsz, lp, d)[:, PAD + N_META:]
```

```python
import functools
import math

import jax
import jax.numpy as jnp
from jax import lax
from jax.experimental import pallas as pl
from jax.experimental.pallas import tpu as pltpu

F32 = jnp.float32
BF16 = jnp.bfloat16

D_MODEL = 1024
DEPTH = 4
N_META = 16
CHUNK = 64
PAD = CHUNK - N_META
NEG = -1e30
LN_EPS = 1e-5
DN_ALPHA = (2.0 * DEPTH) ** 0.25

A_HEADS = 4
A_DQK = D_MODEL // 16
A_DV = D_MODEL // 8
A_GATE_CAP = 15.0
B_CH = D_MODEL // 2
B_GROUP = 16
B_GROUPS = B_CH // B_GROUP
B_STATE = 64
S5_CHUNK = 16
C_HEADS = D_MODEL // 128
C_DK = 128
C_CONV = 4
N_EXPERTS = 32
TOP_K = 4
SWIGLU_LIMIT = 7.0
SWIGLU_ALPHA = 1.702
MOE_BLOCK = 256

LANES = 128
VMEM_LIMIT = 56 * 1024 * 1024


def _cparams(*sem):
    return pltpu.CompilerParams(dimension_semantics=sem, vmem_limit_bytes=VMEM_LIMIT)


def _chunk_tile(lp, target):
    best = CHUNK
    for t in range(CHUNK, target + 1, CHUNK):
        if lp % t == 0:
            best = t
    return best


def _dot(a, b):
    return jnp.dot(a, b, preferred_element_type=F32)


def _dot_nt(a, b):
    return lax.dot_general(a, b, (((1,), (1,)), ((), ())), preferred_element_type=F32)


def _dot_tn(a, b):
    return lax.dot_general(a, b, (((0,), (0,)), ((), ())), preferred_element_type=F32)


def _split3(x):
    hi = x.astype(BF16)
    r1 = x - hi.astype(F32)
    mid = r1.astype(BF16)
    lo = (r1 - mid.astype(F32)).astype(BF16)
    return hi, mid, lo


def _dot01_left(t01, x):
    hi, mid, lo = _split3(x)
    return _dot(t01, hi) + _dot(t01, mid) + _dot(t01, lo)


def _dot01_right(x, t01):
    hi, mid, lo = _split3(x)
    return _dot(hi, t01) + _dot(mid, t01) + _dot(lo, t01)


def _sigmoid(x):
    return 1.0 / (1.0 + jnp.exp(-x))


def _softplus(x):
    return jnp.maximum(x, 0.0) + jnp.log(1.0 + jnp.exp(-jnp.abs(x)))


def _log_sigmoid(x):
    return -_softplus(-x)


def _silu(x):
    return x * _sigmoid(x)


def _gelu_tanh(x):
    c = math.sqrt(2.0 / math.pi)
    return 0.5 * x * (1.0 + jnp.tanh(c * (x + 0.044715 * (x * x * x))))


def _iota(shape, dim):
    return lax.broadcasted_iota(jnp.int32, shape, dim)


def _tri_masks(n):
    r = _iota((n, n), 0)
    c = _iota((n, n), 1)
    return r, c


def _proj_kernel(x_ref, w_ref, wg_ref, wgt_ref, y_ref, gcol_ref, grow_ref):
    xb = x_ref[...].astype(BF16)
    y_ref[...] = _dot(xb, w_ref[...])

    @pl.when(pl.program_id(1) == 0)
    def _():
        gcol_ref[...] = _dot(xb, wg_ref[...])
        grow = _dot_nt(wgt_ref[...], xb)
        for j in range(grow_ref.shape[0]):
            grow_ref[j] = grow[:, j * CHUNK:(j + 1) * CHUNK]


def _in_proj(h, w_main, w_gate, w_gate_t, tm, tn):
    r, d = h.shape
    n = w_main.shape[1]
    cpt = tm // CHUNK
    return pl.pallas_call(
        _proj_kernel,
        grid=(r // tm, n // tn),
        in_specs=[pl.BlockSpec((tm, d), lambda i, j: (i, 0)),
                  pl.BlockSpec((d, tn), lambda i, j: (0, j)),
                  pl.BlockSpec((d, LANES), lambda i, j: (0, 0)),
                  pl.BlockSpec((16, d), lambda i, j: (0, 0))],
        out_specs=[pl.BlockSpec((tm, tn), lambda i, j: (i, j)),
                   pl.BlockSpec((tm, LANES), lambda i, j: (i, 0)),
                   pl.BlockSpec((cpt, 16, CHUNK), lambda i, j: (i, 0, 0))],
        out_shape=[jax.ShapeDtypeStruct((r, n), F32),
                   jax.ShapeDtypeStruct((r, LANES), F32),
                   jax.ShapeDtypeStruct((r // CHUNK, 16, CHUNK), F32)],
        compiler_params=_cparams("parallel", "arbitrary"),
        name="in_proj",
    )(h, w_main, w_gate, w_gate_t)


def _mlstm_kernel(q_ref, k_ref, v_ref, o_ref, gcol_ref, grow_ref, bcol_ref, brow_ref, hn_ref,
                  out_ref, c_scr, m_scr):
    c = pl.program_id(1)

    @pl.when(c == 0)
    def _():
        c_scr[...] = jnp.zeros_like(c_scr)
        m_scr[...] = jnp.zeros_like(m_scr)

    n = CHUNK
    ri, ci = _tri_masks(n)
    causal = ri >= ci
    tri_l = jnp.where(causal, 1.0, 0.0).astype(BF16)
    tri_u = jnp.where(ri <= ci, 1.0, 0.0).astype(BF16)
    first = c == 0
    pad_c = jnp.logical_and(first, _iota((n, 1), 0) < PAD)
    pad_r = jnp.logical_and(first, _iota((1, n), 1) < PAD)

    gc = gcol_ref[...] + bcol_ref[...]
    gc = A_GATE_CAP * jnp.tanh(gc * (1.0 / A_GATE_CAP))
    i_c = jnp.where(pad_c, NEG, gc)
    b_c = _dot01_left(tri_l, jnp.where(pad_c, 0.0, _log_sigmoid(gc)))
    gr = grow_ref[0] + brow_ref[...]
    gr = A_GATE_CAP * jnp.tanh(gr * (1.0 / A_GATE_CAP))
    i_r = jnp.where(pad_r, NEG, gr)
    b_r = _dot01_right(jnp.where(pad_r, 0.0, _log_sigmoid(gr)), tri_u)

    one_col = jnp.where(_iota((n, A_DV), 1) == 0, 1.0, 0.0).astype(BF16)
    heads = range(A_HEADS)
    sl = [slice(h * LANES, (h + 1) * LANES) for h in heads]
    q = [q_ref[:, sl[h]].astype(BF16) for h in heads]
    kf = [k_ref[:, sl[h]] * (A_DQK ** -0.5) for h in heads]
    vext = [jnp.concatenate([v_ref[:, sl[h]].astype(BF16), one_col], axis=1) for h in heads]
    bi = [b_c[:, A_HEADS + h:A_HEADS + h + 1] for h in heads]
    ii = [i_c[:, h:h + 1] for h in heads]
    bj = [b_r[A_HEADS + h:A_HEADS + h + 1, :] for h in heads]
    ij = [i_r[h:h + 1, :] for h in heads]
    b_last = [t[n - 1:n, :] for t in bi]
    a_end = [b_last[h] - bi[h] + ii[h] for h in heads]
    m_loc = [jnp.max(t, axis=0, keepdims=True) for t in a_end]
    m0 = [m_scr[h][0:1, 0:1] for h in heads]
    d_intra = [jnp.where(causal, bi[h] - bj[h] + ij[h], NEG) for h in heads]
    d_inter = [bi[h] + m0[h] for h in heads]
    m_row = [jnp.maximum(jnp.max(d_intra[h], axis=1, keepdims=True), d_inter[h]) for h in heads]
    s = [jnp.exp(d_inter[h] - m_row[h]) for h in heads]
    qk = [(_dot_nt(q[h], kf[h].astype(BF16)) * jnp.exp(d_intra[h] - m_row[h])).astype(BF16) for h in heads]
    cext = [c_scr[h] for h in heads]
    num_ext = [_dot(qk[h], vext[h]) + s[h] * _dot(q[h], cext[h].astype(BF16)) for h in heads]
    w_end = [jnp.exp(a_end[h] - m_loc[h]) for h in heads]
    d_ext = [_dot_tn((kf[h] * w_end[h]).astype(BF16), vext[h]) for h in heads]
    for h in heads:
        m_new = jnp.maximum(b_last[h] + m0[h], m_loc[h])
        c_scr[h] = jnp.exp(b_last[h] + m0[h] - m_new) * cext[h] + jnp.exp(m_loc[h] - m_new) * d_ext[h]
        m_scr[h] = jnp.broadcast_to(m_new, (8, LANES))
    for h in heads:
        den = num_ext[h][:, A_DV:A_DV + 1]
        hh = num_ext[h][:, :A_DV] / jnp.maximum(jnp.abs(den), jnp.exp(-m_row[h]))
        mu = jnp.mean(hh, axis=1, keepdims=True)
        hc = hh - mu
        var = jnp.mean(hc * hc, axis=1, keepdims=True)
        hnorm = hc * lax.rsqrt(var + 1e-6) * hn_ref[:, sl[h]]
        out_ref[:, sl[h]] = _sigmoid(o_ref[:, sl[h]]) * hnorm


def _mlstm(y, gcol, grow3, bias_col, bias_row, head_norm, bsz, nc):
    r = y.shape[0]
    w = A_HEADS * LANES
    row = lambda b, c: b * nc + c
    return pl.pallas_call(
        _mlstm_kernel,
        grid=(bsz, nc),
        in_specs=[pl.BlockSpec((CHUNK, w), lambda b, c: (row(b, c), 0)),
                  pl.BlockSpec((CHUNK, w), lambda b, c: (row(b, c), 1)),
                  pl.BlockSpec((CHUNK, w), lambda b, c: (row(b, c), 2)),
                  pl.BlockSpec((CHUNK, w), lambda b, c: (row(b, c), 3)),
                  pl.BlockSpec((CHUNK, LANES), lambda b, c: (row(b, c), 0)),
                  pl.BlockSpec((1, 16, CHUNK), lambda b, c: (row(b, c), 0, 0)),
                  pl.BlockSpec((1, LANES), lambda b, c: (0, 0)),
                  pl.BlockSpec((16, CHUNK), lambda b, c: (0, 0)),
                  pl.BlockSpec((1, w), lambda b, c: (0, 0))],
        out_specs=pl.BlockSpec((CHUNK, w), lambda b, c: (row(b, c), 0)),
        out_shape=jax.ShapeDtypeStruct((r, w), F32),
        scratch_shapes=[pltpu.VMEM((A_HEADS, LANES, 2 * LANES), F32),
                        pltpu.VMEM((A_HEADS, 8, LANES), F32)],
        compiler_params=_cparams("parallel", "arbitrary"),
        name="mlstm",
    )(y, y, y, y, gcol, grow3, bias_col, bias_row, head_norm)


def _s5_params(a_re, a_im, log_step, b_re, b_im, c_re, c_im, d_skip):
    hp = lax.Precision.HIGHEST
    g, p = a_re.shape
    dt = jnp.exp(log_step)[:, None]
    mag = jnp.exp(a_re * dt)
    lb_re, lb_im = mag * jnp.cos(a_im * dt), mag * jnp.sin(a_im * dt)
    inv = 1.0 / (a_re * a_re + a_im * a_im)
    zr, zi = lb_re - 1.0, lb_im
    fr = (zr * a_re + zi * a_im) * inv
    fi = (zi * a_re - zr * a_im) * inv
    bb_re = fr[..., None] * b_re - fi[..., None] * b_im
    bb_im = fr[..., None] * b_im + fi[..., None] * b_re
    n = S5_CHUNK
    tau = jnp.arange(n + 1, dtype=F32)[:, None, None]
    pm = jnp.exp(tau * (a_re * dt))
    pr, pi = pm * jnp.cos(tau * (a_im * dt)), pm * jnp.sin(tau * (a_im * dt))
    e_re = pr[..., None] * bb_re - pi[..., None] * bb_im
    e_im = pr[..., None] * bb_im + pi[..., None] * bb_re
    kern = (jnp.einsum('gcp,tgpd->tgcd', c_re, e_re[:n], precision=hp)
            - jnp.einsum('gcp,tgpd->tgcd', c_im, e_im[:n], precision=hp))
    idx = jnp.arange(n)
    diff = idx[None, :] - idx[:, None]
    kd = jnp.where((diff >= 0)[:, :, None, None, None], kern[jnp.clip(diff, 0, n - 1)], 0.0)
    m_t = kd.transpose(2, 0, 4, 1, 3).reshape(g, n * B_GROUP, n * B_GROUP)
    w_re = e_re[n - 1 - idx].transpose(1, 0, 3, 2)
    w_im = e_im[n - 1 - idx].transpose(1, 0, 3, 2)
    w_t = jnp.concatenate([w_re, w_im], axis=-1).reshape(g, n * B_GROUP, 2 * p)
    f_re = c_re[None] * pr[1:, :, None, :] - c_im[None] * pi[1:, :, None, :]
    f_im = c_re[None] * pi[1:, :, None, :] + c_im[None] * pr[1:, :, None, :]
    v_t = jnp.concatenate([f_re.transpose(1, 3, 0, 2), -f_im.transpose(1, 3, 0, 2)], axis=1)
    v_t = v_t.reshape(g, 2 * p, n * B_GROUP)
    lam_a = jnp.concatenate([pr[n], pr[n]], axis=-1)[:, None, :]
    lam_b = jnp.concatenate([-pi[n], pi[n]], axis=-1)[:, None, :]
    d_flat = jnp.tile(d_skip, (1, n))[:, None, :]
    return m_t.astype(BF16), w_t.astype(BF16), v_t.astype(BF16), lam_a, lam_b, d_flat


def _s5_in_kernel(u_ref, mt_ref, wt_ref, d_ref, yi_ref, z_ref):
    u = u_ref[0]
    ub = u.astype(BF16)
    yi_ref[0] = _dot(ub, mt_ref[0]) + d_ref[0] * u
    z_ref[0] = _dot(ub, wt_ref[0])


def _s5_scan_kernel(z_ref, la_ref, lb_ref, x0_ref, x_scr, *, per, bsz):
    @pl.when(pl.program_id(0) == 0)
    def _():
        x_scr[...] = jnp.zeros_like(x_scr)

    la = la_ref[...]
    lb = lb_ref[...]
    x = x_scr[...]
    g, _, st = x.shape
    for i in range(per):
        rows = slice(i * bsz, (i + 1) * bsz)
        x0_ref[:, rows, :] = x
        swapped = pltpu.roll(x.reshape(g * bsz, st), B_STATE, 1).reshape(g, bsz, st)
        x = la * x + lb * swapped + z_ref[:, rows, :]
    x_scr[...] = x


def _s5_out_kernel(yi_ref, x0_ref, vt_ref, y_ref):
    y_ref[0] = yi_ref[0] + _dot(x0_ref[0].astype(BF16), vt_ref[0])


def _s5(u3, params, bsz):
    m_t, w_t, v_t, lam_a, lam_b, d_flat = params
    g, rc, wide = u3.shape
    st = 2 * B_STATE
    gspec = lambda shape: pl.BlockSpec((1,) + shape, lambda i: (i, 0, 0))
    yi, z = pl.pallas_call(
        _s5_in_kernel,
        grid=(g,),
        in_specs=[gspec((rc, wide)), gspec((wide, wide)), gspec((wide, st)), gspec((1, wide))],
        out_specs=[gspec((rc, wide)), gspec((rc, st))],
        out_shape=[jax.ShapeDtypeStruct((g, rc, wide), F32), jax.ShapeDtypeStruct((g, rc, st), F32)],
        compiler_params=_cparams("parallel"),
        name="s5_in",
    )(u3, m_t, w_t, d_flat)
    n_steps = rc // bsz
    per = max(p for p in range(1, 17) if n_steps % p == 0)
    x0 = pl.pallas_call(
        functools.partial(_s5_scan_kernel, per=per, bsz=bsz),
        grid=(n_steps // per,),
        in_specs=[pl.BlockSpec((g, per * bsz, st), lambda i: (0, i, 0)),
                  pl.BlockSpec((g, 1, st), lambda i: (0, 0, 0)),
                  pl.BlockSpec((g, 1, st), lambda i: (0, 0, 0))],
        out_specs=pl.BlockSpec((g, per * bsz, st), lambda i: (0, i, 0)),
        out_shape=jax.ShapeDtypeStruct((g, rc, st), F32),
        scratch_shapes=[pltpu.VMEM((g, bsz, st), F32)],
        compiler_params=_cparams("arbitrary"),
        name="s5_scan",
    )(z, lam_a, lam_b)
    return pl.pallas_call(
        _s5_out_kernel,
        grid=(g,),
        in_specs=[gspec((rc, wide)), gspec((rc, st)), gspec((st, wide))],
        out_specs=gspec((rc, wide)),
        out_shape=jax.ShapeDtypeStruct((g, rc, wide), F32),
        compiler_params=_cparams("parallel"),
        name="s5_out",
    )(yi, x0, v_t)


def _res_ln(h, mix, g, b, pad_rows):
    z = DN_ALPHA * h + mix
    mu = jnp.mean(z, axis=1, keepdims=True)
    zc = z - mu
    var = jnp.mean(zc * zc, axis=1, keepdims=True)
    out = zc * lax.rsqrt(var + LN_EPS) * g + b
    return jnp.where(pad_rows, 0.0, out)


def _pad_rows_mask(tm, lp):
    start = (pl.program_id(0) * tm) % lp
    pos = start + _iota((tm, 1), 0)
    pos = jnp.where(pos >= lp, pos - lp, pos)
    return pos < PAD


def _even_out_kernel(ha_ref, ys_ref, h_ref, wglu_ref, bglu_ref, wout_ref, g_ref, b_ref, out_ref,
                     *, lp):
    tm = h_ref.shape[0]
    yb = _gelu_tanh(ys_ref[...])
    hb = yb * _sigmoid(_dot(yb.astype(BF16), wglu_ref[...]) + bglu_ref[...])
    av = ha_ref.shape[1]
    mix = _dot(ha_ref[...].astype(BF16), wout_ref[:av, :]) + _dot(hb.astype(BF16), wout_ref[av:, :])
    out_ref[...] = _res_ln(h_ref[...], mix, g_ref[...], b_ref[...], _pad_rows_mask(tm, lp))


def _even_out(ha, ys, h, w_glu, b_glu, w_out, ln_g, ln_b, tm, lp):
    r, d = h.shape
    av, bc = ha.shape[1], ys.shape[1]
    full = lambda shape: pl.BlockSpec(shape, lambda i: (0, 0))
    return pl.pallas_call(
        functools.partial(_even_out_kernel, lp=lp),
        grid=(r // tm,),
        in_specs=[pl.BlockSpec((tm, av), lambda i: (i, 0)),
                  pl.BlockSpec((tm, bc), lambda i: (i, 0)),
                  pl.BlockSpec((tm, d), lambda i: (i, 0)),
                  full((bc, bc)), full((1, bc)), full((av + bc, d)), full((1, d)), full((1, d))],
        out_specs=pl.BlockSpec((tm, d), lambda i: (i, 0)),
        out_shape=jax.ShapeDtypeStruct((r, d), F32),
        compiler_params=_cparams("parallel"),
        name="even_out",
    )(ha, ys, h, w_glu, b_glu, w_out, ln_g, ln_b)


def _gdn_kernel(q_ref, k_ref, v_ref, z_ref, gcol_ref, grow_ref, conv_ref, pcol_ref, prow_ref,
                nw_ref, out_ref, s_scr, carry_scr, ext_scr):
    c = pl.program_id(1)

    @pl.when(c == 0)
    def _():
        s_scr[...] = jnp.zeros_like(s_scr)
        carry_scr[...] = jnp.zeros_like(carry_scr)

    n = CHUNK
    ri, ci = _tri_masks(n)
    incl = ri >= ci
    strict = ri > ci
    tri_l = jnp.where(incl, 1.0, 0.0).astype(BF16)
    tri_u = jnp.where(ri <= ci, 1.0, 0.0).astype(BF16)
    first = c == 0
    pad_c = jnp.logical_and(first, _iota((n, 1), 0) < PAD)
    pad_r = jnp.logical_and(first, _iota((1, n), 1) < PAD)

    gc = gcol_ref[...]
    beta_c = jnp.where(pad_c, 0.0, _sigmoid(gc))
    g_c = jnp.where(pad_c, 0.0, pcol_ref[1:2, :] * _softplus(gc + pcol_ref[0:1, :]))
    gcum_c = _dot01_left(tri_l, g_c)
    gr = grow_ref[0]
    g_r = jnp.where(pad_r, 0.0, prow_ref[1] * _softplus(gr + prow_ref[0]))
    gcum_r = _dot01_right(g_r, tri_u)

    width = q_ref.shape[1]
    ext_scr[0:8, :] = carry_scr[...]
    for j, ref in enumerate((q_ref, k_ref, v_ref)):
        ext_scr[8:8 + n, j * width:(j + 1) * width] = ref[...]
    carry_scr[...] = ext_scr[n:n + 8, :]

    def conv(col):
        acc = conv_ref[0:1, col] * ext_scr[5:5 + n, col]
        for j in range(1, C_CONV):
            acc = acc + conv_ref[j:j + 1, col] * ext_scr[5 + j:5 + j + n, col]
        return _silu(acc)

    def l2n(x):
        return x * lax.rsqrt(jnp.sum(x * x, axis=1, keepdims=True) + 1e-6)

    heads = range(C_HEADS)
    lanes = lambda base, h: slice(base + h * LANES, base + (h + 1) * LANES)
    q = [l2n(conv(lanes(0, h))) * (C_DK ** -0.5) for h in heads]
    k = [l2n(conv(lanes(width, h))) for h in heads]
    v = [conv(lanes(2 * width, h)) for h in heads]
    beta = [beta_c[:, h:h + 1] for h in heads]
    gi = [gcum_c[:, C_HEADS + h:C_HEADS + h + 1] for h in heads]
    gj = [gcum_r[C_HEADS + h:C_HEADS + h + 1, :] for h in heads]
    g_last = [g[n - 1:n, :] for g in gi]
    decay = [jnp.where(incl, jnp.exp(jnp.where(incl, gi[h] - gj[h], 0.0)), 0.0) for h in heads]
    eg = [jnp.exp(g) for g in gi]
    kb = [t.astype(BF16) for t in k]
    qkk = [_dot_nt(jnp.concatenate([q[h].astype(BF16), kb[h]], axis=0), kb[h]) for h in heads]
    attn = [(qkk[h][:n] * decay[h]).astype(BF16) for h in heads]
    ab = [jnp.where(strict, -(beta[h] * qkk[h][n:] * decay[h]), 0.0).astype(BF16) for h in heads]
    x = [jnp.concatenate([beta[h] * v[h], (beta[h] * eg[h]) * k[h]], axis=1) for h in heads]
    x = [x[h] + _dot(ab[h], x[h].astype(BF16)) for h in heads]
    for _ in range(5):
        ab = [_dot(t, t).astype(BF16) for t in ab]
        x = [x[h] + _dot(ab[h], x[h].astype(BF16)) for h in heads]
    s0 = [s_scr[h] for h in heads]
    q_dec = [(q[h] * eg[h]).astype(BF16) for h in heads]
    k_dec = [(k[h] * jnp.exp(g_last[h] - gi[h])).astype(BF16) for h in heads]
    ws_qs = [_dot(jnp.concatenate([x[h][:, LANES:].astype(BF16), q_dec[h]], axis=0), s0[h].astype(BF16))
             for h in heads]
    vb = [(x[h][:, :LANES] - ws_qs[h][:n]).astype(BF16) for h in heads]
    o = [ws_qs[h][n:] + _dot(attn[h], vb[h]) for h in heads]
    for h in heads:
        s_scr[h] = jnp.exp(g_last[h]) * s0[h] + _dot_tn(k_dec[h], vb[h])
    for h in heads:
        on = o[h] * lax.rsqrt(jnp.mean(o[h] * o[h], axis=1, keepdims=True) + 1e-6) * nw_ref[...]
        out_ref[:, lanes(0, h)] = on * _silu(z_ref[:, lanes(0, h)])


def _gdn(y, gcol, grow3, conv_w, pcol, prow, norm_w, bsz, nc):
    r = y.shape[0]
    w = C_HEADS * C_DK
    row = lambda b, c: b * nc + c
    return pl.pallas_call(
        _gdn_kernel,
        grid=(bsz, nc),
        in_specs=[pl.BlockSpec((CHUNK, w), lambda b, c: (row(b, c), 0)),
                  pl.BlockSpec((CHUNK, w), lambda b, c: (row(b, c), 1)),
                  pl.BlockSpec((CHUNK, w), lambda b, c: (row(b, c), 2)),
                  pl.BlockSpec((CHUNK, w), lambda b, c: (row(b, c), 3)),
                  pl.BlockSpec((CHUNK, LANES), lambda b, c: (row(b, c), 0)),
                  pl.BlockSpec((1, 16, CHUNK), lambda b, c: (row(b, c), 0, 0)),
                  pl.BlockSpec((C_CONV, 3 * w), lambda b, c: (0, 0)),
                  pl.BlockSpec((2, LANES), lambda b, c: (0, 0)),
                  pl.BlockSpec((2, 16, CHUNK), lambda b, c: (0, 0, 0)),
                  pl.BlockSpec((1, LANES), lambda b, c: (0, 0))],
        out_specs=pl.BlockSpec((CHUNK, w), lambda b, c: (row(b, c), 0)),
        out_shape=jax.ShapeDtypeStruct((r, w), F32),
        scratch_shapes=[pltpu.VMEM((C_HEADS, C_DK, C_DK), F32),
                        pltpu.VMEM((8, 3 * w), F32),
                        pltpu.VMEM((CHUNK + 8, 3 * w), F32)],
        compiler_params=_cparams("parallel", "arbitrary"),
        name="gdn",
    )(y, y, y, y, gcol, grow3, conv_w, pcol, prow, norm_w)


def _odd_out_kernel(o_ref, h_ref, wout_ref, g_ref, b_ref, out_ref, *, lp):
    tm = h_ref.shape[0]
    mix = _dot(o_ref[...].astype(BF16), wout_ref[...])
    out_ref[...] = _res_ln(h_ref[...], mix, g_ref[...], b_ref[...], _pad_rows_mask(tm, lp))


def _odd_out(o, h, w_out, ln_g, ln_b, tm, lp):
    r, d = h.shape
    full = lambda shape: pl.BlockSpec(shape, lambda i: (0, 0))
    return pl.pallas_call(
        functools.partial(_odd_out_kernel, lp=lp),
        grid=(r // tm,),
        in_specs=[pl.BlockSpec((tm, d), lambda i: (i, 0)),
                  pl.BlockSpec((tm, d), lambda i: (i, 0)),
                  full((d, d)), full((1, d)), full((1, d))],
        out_specs=pl.BlockSpec((tm, d), lambda i: (i, 0)),
        out_shape=jax.ShapeDtypeStruct((r, d), F32),
        compiler_params=_cparams("parallel"),
        name="odd_out",
    )(o, h, w_out, ln_g, ln_b)


def _router_kernel(x_ref, w_ref, b_ref, meta_ref, gate_ref, cnt_ref, base_scr):
    @pl.when(pl.program_id(0) == 0)
    def _():
        base_scr[...] = jnp.zeros_like(base_scr)

    tm = x_ref.shape[0]
    logits = _dot(x_ref[...].astype(BF16), w_ref[...]) + b_ref[...]
    lane = _iota((tm, LANES), 1)
    lane_f = lane.astype(F32)
    work = logits
    vals, sels = [], []
    onehot = jnp.zeros((tm, LANES), F32)
    for _ in range(TOP_K):
        m = jnp.max(work, axis=1, keepdims=True)
        idx = jnp.min(jnp.where(work == m, lane_f, float(LANES)), axis=1, keepdims=True)
        sel = lane_f == idx
        vals.append(m)
        sels.append((sel, idx))
        onehot = onehot + jnp.where(sel, 1.0, 0.0)
        work = jnp.where(sel, -jnp.inf, work)
    ri, ci = _tri_masks(tm)
    tri = jnp.where(ri > ci, 1.0, 0.0).astype(BF16)
    before = _dot(tri, onehot.astype(BF16)) + base_scr[0:1, :]
    base_scr[...] = base_scr[...] + jnp.sum(onehot, axis=0, keepdims=True)
    cnt_ref[...] = base_scr[...]
    es = [jnp.exp(v - vals[0]) for v in vals]
    tot = es[0] + es[1] + es[2] + es[3]
    meta = jnp.zeros((tm, LANES), jnp.int32)
    gate = jnp.zeros((tm, LANES), F32)
    for k, (sel, idx) in enumerate(sels):
        rank = jnp.sum(jnp.where(sel, before, 0.0), axis=1, keepdims=True)
        meta = jnp.where(lane == k, idx.astype(jnp.int32), meta)
        meta = jnp.where(lane == TOP_K + k, rank.astype(jnp.int32), meta)
        gate = jnp.where(lane == k, es[k] / tot, gate)
    meta_ref[...] = meta
    gate_ref[...] = gate


def _router(h, w_r, b_r, tm):
    r, d = h.shape
    return pl.pallas_call(
        _router_kernel,
        grid=(r // tm,),
        in_specs=[pl.BlockSpec((tm, d), lambda i: (i, 0)),
                  pl.BlockSpec((d, LANES), lambda i: (0, 0)),
                  pl.BlockSpec((1, LANES), lambda i: (0, 0))],
        out_specs=[pl.BlockSpec((tm, LANES), lambda i: (i, 0)),
                   pl.BlockSpec((tm, LANES), lambda i: (i, 0)),
                   pl.BlockSpec((8, LANES), lambda i: (0, 0))],
        out_shape=[jax.ShapeDtypeStruct((r, LANES), jnp.int32),
                   jax.ShapeDtypeStruct((r, LANES), F32),
                   jax.ShapeDtypeStruct((8, LANES), F32)],
        scratch_shapes=[pltpu.VMEM((8, LANES), F32)],
        compiler_params=_cparams("arbitrary"),
        name="router",
    )(h, w_r, b_r)


ROW_DMA_UNROLL = 8
ROW_DMA_WAIT_GROUP = 64


def _issue_row_copies(make_copy, tm):
    def issue(i, carry):
        for u in range(ROW_DMA_UNROLL):
            for k in range(TOP_K):
                make_copy(i * ROW_DMA_UNROLL + u, k).start(priority=(u * TOP_K + k) % 2)
        return carry

    lax.fori_loop(0, tm // ROW_DMA_UNROLL, issue, 0)


def _wait_row_copies(make_copy, tm):
    def drain(i, carry):
        for _ in range(ROW_DMA_WAIT_GROUP):
            make_copy(0, 0).wait()
        return carry

    lax.fori_loop(0, tm * TOP_K // ROW_DMA_WAIT_GROUP, drain, 0)


def _dispatch_kernel(dest_ref, x_ref, xs_in, xs_hbm, sem, *, tm):
    del xs_in

    def make_copy(t, k):
        return pltpu.make_async_copy(x_ref.at[pl.ds(t, 1)],
                                     xs_hbm.at[pl.ds(dest_ref[t * TOP_K + k], 1)], sem)

    _issue_row_copies(make_copy, tm)
    _wait_row_copies(make_copy, tm)


def _dispatch(h, dest_flat, n_slots, tm):
    r, d = h.shape
    xs0 = jnp.zeros((n_slots, d), h.dtype)
    return pl.pallas_call(
        functools.partial(_dispatch_kernel, tm=tm),
        grid=(r // tm,),
        in_specs=[pl.BlockSpec((tm * TOP_K,), lambda i: (i,), memory_space=pltpu.SMEM),
                  pl.BlockSpec((tm, d), lambda i: (i, 0)),
                  pl.BlockSpec(memory_space=pl.ANY)],
        out_specs=pl.BlockSpec(memory_space=pl.ANY),
        out_shape=jax.ShapeDtypeStruct((n_slots, d), h.dtype),
        scratch_shapes=[pltpu.SemaphoreType.DMA],
        input_output_aliases={2: 0},
        compiler_params=_cparams("arbitrary"),
        name="moe_dispatch",
    )(dest_flat, h, xs0)


def _expert_kernel(be_ref, nu_ref, x_ref, wgu_ref, bgu_ref, wd_ref, bd_ref, out_ref, wgu_scr, wd_scr):
    b = pl.program_id(0)
    prev = be_ref[jnp.maximum(b - 1, 0)]
    fresh = jnp.logical_or(b == 0, be_ref[b] != prev)
    tile = 2 * LANES
    n_tiles = wgu_ref.shape[2] // tile

    @pl.when(fresh)
    def _():
        r = _iota((tile, tile), 0)
        c = _iota((tile, tile), 1)
        src = jnp.where(c < LANES, 2 * c, 2 * (c - LANES) + 1)
        perm = jnp.where(r == src, 1.0, 0.0).astype(BF16)
        for j in range(n_tiles):
            cols = slice(j * tile, (j + 1) * tile)
            wgu_scr[:, cols] = _dot(wgu_ref[0, :, cols].astype(BF16), perm).astype(BF16)
        wd_scr[...] = wd_ref[0].astype(BF16)

    @pl.when(b < nu_ref[0])
    def _():
        xb = x_ref[...].astype(BF16)
        h = _dot(xb, wgu_scr[...]) + bgu_ref[0]
        acts = []
        for j in range(n_tiles):
            gate = jnp.minimum(h[:, j * tile:j * tile + LANES], SWIGLU_LIMIT)
            up = jnp.clip(h[:, j * tile + LANES:(j + 1) * tile], -SWIGLU_LIMIT, SWIGLU_LIMIT)
            acts.append(((up + 1.0) * gate * _sigmoid(SWIGLU_ALPHA * gate)).astype(BF16))
        act = jnp.concatenate(acts, axis=1)
        out_ref[...] = _dot(act, wd_scr[...]) + bd_ref[0]

    @pl.when(b >= nu_ref[0])
    def _():
        out_ref[...] = jnp.zeros_like(out_ref)


def _experts(xs, block_e, n_used, w_gu, b_gu, w_d, b_d):
    n_slots, d = xs.shape
    n_blocks = n_slots // MOE_BLOCK
    de2 = w_gu.shape[2]
    de = w_d.shape[1]
    grid_spec = pltpu.PrefetchScalarGridSpec(
        num_scalar_prefetch=2,
        grid=(n_blocks,),
        in_specs=[pl.BlockSpec((MOE_BLOCK, d), lambda b, be, nu: (b, 0)),
                  pl.BlockSpec((1, d, de2), lambda b, be, nu: (be[b], 0, 0)),
                  pl.BlockSpec((1, 1, de2), lambda b, be, nu: (be[b], 0, 0)),
                  pl.BlockSpec((1, de, d), lambda b, be, nu: (be[b], 0, 0)),
                  pl.BlockSpec((1, 1, d), lambda b, be, nu: (be[b], 0, 0))],
        out_specs=pl.BlockSpec((MOE_BLOCK, d), lambda b, be, nu: (b, 0)),
        scratch_shapes=[pltpu.VMEM((d, de2), BF16), pltpu.VMEM((de, d), BF16)],
    )
    return pl.pallas_call(
        _expert_kernel,
        grid_spec=grid_spec,
        out_shape=jax.ShapeDtypeStruct((n_slots, d), F32),
        compiler_params=_cparams("arbitrary"),
        name="moe_experts",
    )(block_e, n_used, xs, w_gu, b_gu, w_d, b_d)


def _combine_kernel(dest_ref, gate_ref, h_ref, g_ref, b_ref, eo_hbm, out_ref, buf, sem,
                    *, lp):
    tm = h_ref.shape[0]

    def make_copy(t, k):
        return pltpu.make_async_copy(eo_hbm.at[pl.ds(dest_ref[t * TOP_K + k], 1)],
                                     buf.at[k, pl.ds(t, 1)], sem)

    _issue_row_copies(make_copy, tm)
    _wait_row_copies(make_copy, tm)

    gates = gate_ref[...]
    ffn = gates[:, 0:1] * buf[0]
    for k in range(1, TOP_K):
        ffn = ffn + gates[:, k:k + 1] * buf[k]
    out_ref[...] = _res_ln(h_ref[...], ffn, g_ref[...], b_ref[...], _pad_rows_mask(tm, lp))


def _combine(eo, dest_flat, gates, h, ln_g, ln_b, tm, lp):
    r, d = h.shape
    full = lambda shape: pl.BlockSpec(shape, lambda i: (0, 0))
    return pl.pallas_call(
        functools.partial(_combine_kernel, lp=lp),
        grid=(r // tm,),
        in_specs=[pl.BlockSpec((tm * TOP_K,), lambda i: (i,), memory_space=pltpu.SMEM),
                  pl.BlockSpec((tm, LANES), lambda i: (i, 0)),
                  pl.BlockSpec((tm, d), lambda i: (i, 0)),
                  full((1, d)), full((1, d)),
                  pl.BlockSpec(memory_space=pl.ANY)],
        out_specs=pl.BlockSpec((tm, d), lambda i: (i, 0)),
        out_shape=jax.ShapeDtypeStruct((r, d), F32),
        scratch_shapes=[pltpu.VMEM((TOP_K, tm, d), F32), pltpu.SemaphoreType.DMA],
        compiler_params=_cparams("arbitrary"),
        name="moe_combine",
    )(dest_flat, gates, h, ln_g, ln_b, eo)


def _moe(h, w_router, b_router, w_gu, b_gu, w_d, b_d, ln_g, ln_b, tm, lp):
    r, d = h.shape
    w_r = jnp.pad(w_router, ((0, 0), (0, LANES - N_EXPERTS))).astype(BF16)
    b_r = jnp.pad(b_router, (0, LANES - N_EXPERTS), constant_values=NEG)[None, :]
    meta, gates, cnt = _router(h, w_r, b_r, tm)
    counts = cnt[0, :N_EXPERTS].astype(jnp.int32)
    padded = (counts + MOE_BLOCK - 1) // MOE_BLOCK * MOE_BLOCK
    ends_p = jnp.cumsum(padded)
    pstart = ends_p - padded
    n_blocks = -(-(r * TOP_K) // MOE_BLOCK) + N_EXPERTS
    n_slots = n_blocks * MOE_BLOCK
    block_start = jnp.arange(n_blocks, dtype=jnp.int32) * MOE_BLOCK
    block_e = jnp.minimum(jnp.sum((ends_p[None, :] <= block_start[:, None]).astype(jnp.int32), axis=1),
                          N_EXPERTS - 1)
    n_used = (ends_p[-1:] // MOE_BLOCK).astype(jnp.int32)
    dest = (pstart[meta[:, :TOP_K]] + meta[:, TOP_K:2 * TOP_K]).reshape(-1)
    xs = _dispatch(h, dest, n_slots, tm)
    half = w_gu.shape[2] // 2
    b_gu_t = b_gu.reshape(N_EXPERTS, half // LANES, LANES, 2).transpose(0, 1, 3, 2)
    b_gu_t = b_gu_t.reshape(N_EXPERTS, 1, 2 * half)
    eo = _experts(xs, block_e, n_used, w_gu, b_gu_t, w_d, b_d[:, None, :])
    return _combine(eo, dest, gates, h, ln_g, ln_b, tm, lp)


def _pad_heads(w, heads, dim):
    k = w.shape[0]
    return jnp.pad(w.reshape(k, heads, dim), ((0, 0), (0, 0), (0, LANES - dim))).reshape(k, heads * LANES)


def _gate_weights(wg):
    k, n = wg.shape
    return (jnp.pad(wg, ((0, 0), (0, LANES - n))).astype(BF16),
            jnp.pad(wg.T, ((0, 16 - n), (0, 0))).astype(BF16))


def _even_layer(h, p, bsz, nc, tm, lp):
    r = h.shape[0]
    w_in = p['w_in']
    aq, av = A_HEADS * A_DQK, A_HEADS * A_DV
    o0 = 2 * aq + 2 * av
    w_main = jnp.concatenate([_pad_heads(w_in[:, :aq], A_HEADS, A_DQK),
                              _pad_heads(w_in[:, aq:2 * aq], A_HEADS, A_DQK),
                              w_in[:, 2 * aq:2 * aq + av], w_in[:, 2 * aq + av:o0],
                              w_in[:, o0 + 2 * A_HEADS:]], axis=1).astype(BF16)
    w_gate, w_gate_t = _gate_weights(w_in[:, o0:o0 + 2 * A_HEADS])
    y, gcol, grow = _in_proj(h, w_main, w_gate, w_gate_t, _chunk_tile(r // bsz, 704), w_main.shape[1])
    gb = p['gate_bias']
    bias_col = jnp.pad(gb, (0, LANES - gb.shape[0]))[None, :]
    bias_row = jnp.broadcast_to(jnp.pad(gb, (0, 16 - gb.shape[0]))[:, None], (16, CHUNK))
    ha = _mlstm(y, gcol, grow, bias_col, bias_row, p['head_norm'][None, :], bsz, nc)
    nch = r // bsz // S5_CHUNK
    u = y[:, 4 * A_HEADS * LANES:]
    u3 = u.reshape(bsz, nch, S5_CHUNK, B_GROUPS, B_GROUP).transpose(3, 1, 0, 2, 4)
    u3 = u3.reshape(B_GROUPS, nch * bsz, S5_CHUNK * B_GROUP)
    y3 = _s5(u3, _s5_params(p['a_re'], p['a_im'], p['log_step'], p['b_re'], p['b_im'],
                            p['c_re'], p['c_im'], p['d']), bsz)
    ys = y3.reshape(B_GROUPS, nch, bsz, S5_CHUNK, B_GROUP).transpose(2, 1, 3, 0, 4).reshape(r, B_CH)
    return _even_out(ha, ys, h, p['w_glu'].astype(BF16), p['b_glu'][None, :], p['w_out'].astype(BF16),
                     p['ln_g'][None, :], p['ln_b'][None, :], tm, lp)


def _odd_layer(h, p, bsz, nc, tm, lp):
    w_in = p['w_in']
    cw = C_HEADS * C_DK
    w_main = w_in[:, :4 * cw].astype(BF16)
    w_gate, w_gate_t = _gate_weights(w_in[:, 4 * cw:])
    y, gcol, grow = _in_proj(h, w_main, w_gate, w_gate_t, _chunk_tile(h.shape[0] // bsz, 704),
                             w_main.shape[1] // 2)
    neg_a = -jnp.exp(p['a_log'])
    zeros = jnp.zeros((C_HEADS,), F32)
    dt16 = jnp.concatenate([zeros, p['dt_bias']])
    na16 = jnp.concatenate([zeros, neg_a])
    pcol = jnp.pad(jnp.stack([dt16, na16]), ((0, 0), (0, LANES - 16)))
    prow = jnp.broadcast_to(jnp.stack([dt16, na16])[:, :, None], (2, 16, CHUNK))
    o = _gdn(y, gcol, grow, p['conv'], pcol, prow, p['norm'][None, :], bsz, nc)
    return _odd_out(o, h, p['w_out'].astype(BF16), p['ln_g'][None, :], p['ln_b'][None, :], tm, lp)


def kernel(x, meta_tokens, ln_g, ln_b, ev_w_in, ev_gate_bias, ev_head_norm, s5_a_re, s5_a_im, s5_log_step, s5_b_re, s5_b_im, s5_c_re, s5_c_im, s5_d, s5_w_glu, s5_b_glu, ev_w_out, od_w_in, od_conv, od_a_log, od_dt_bias, od_norm, od_w_out, moe_w_router, moe_b_router, moe_w_gate_up, moe_b_gate_up, moe_w_down, moe_b_down):
    bsz, seq, d = x.shape
    lp = PAD + N_META + seq
    assert lp % CHUNK == 0 and d == D_MODEL
    nc = lp // CHUNK
    r = bsz * lp
    assert r % MOE_BLOCK == 0 and lp >= MOE_BLOCK
    tm = 512 if (r % 512 == 0 and lp >= 512) else MOE_BLOCK
    meta =jnp.broadcast_to(meta_tokens[None], (bsz, N_META, d)).astype(x.dtype)
    h = jnp.concatenate([jnp.zeros((bsz, PAD, d), x.dtype), meta, x], axis=1).reshape(bsz * lp, d)
    for layer in range(ln_g.shape[0]):
        j = layer // 2
        if layer % 2 == 0:
            p = dict(w_in=ev_w_in[j], gate_bias=ev_gate_bias[j], head_norm=ev_head_norm[j],
                     a_re=s5_a_re[j], a_im=s5_a_im[j], log_step=s5_log_step[j], b_re=s5_b_re[j],
                     b_im=s5_b_im[j], c_re=s5_c_re[j], c_im=s5_c_im[j], d=s5_d[j],
                     w_glu=s5_w_glu[j], b_glu=s5_b_glu[j], w_out=ev_w_out[j],
                     ln_g=ln_g[layer, 0], ln_b=ln_b[layer, 0])
            h = _even_layer(h, p, bsz, nc, tm, lp)
        else:
            p = dict(w_in=od_w_in[j], conv=od_conv[j], a_log=od_a_log[j], dt_bias=od_dt_bias[j],
                     norm=od_norm[j], w_out=od_w_out[j], ln_g=ln_g[layer, 0], ln_b=ln_b[layer, 0])
            h = _odd_layer(h, p, bsz, nc, tm, lp)
        h = _moe(h, moe_w_router[layer], moe_b_router[layer], moe_w_gate_up[layer],
                 moe_b_gate_up[layer], moe_w_down[layer], moe_b_down[layer],
                 ln_g[layer, 1][None, :], ln_b[layer, 1][None, :], MOE_BLOCK, lp)
    return h.reshape(bsz, lp, d)[:, PAD + N_META:]
```

```python
import functools
import math

import jax
import jax.numpy as jnp
from jax import lax
from jax.experimental import pallas as pl
from jax.experimental.pallas import tpu as pltpu

F32 = jnp.float32
BF16 = jnp.bfloat16

D_MODEL = 1024
DEPTH = 4
N_META = 16
CHUNK = 64
PAD = CHUNK - N_META
NEG = -1e30
LN_EPS = 1e-5
DN_ALPHA = (2.0 * DEPTH) ** 0.25

A_HEADS = 4
A_DQK = D_MODEL // 16
A_DV = D_MODEL // 8
A_GATE_CAP = 15.0
B_CH = D_MODEL // 2
B_GROUP = 16
B_GROUPS = B_CH // B_GROUP
B_STATE = 64
S5_CHUNK = 16
C_HEADS = D_MODEL // 128
C_DK = 128
C_CONV = 4
N_EXPERTS = 32
TOP_K = 4
SWIGLU_LIMIT = 7.0
SWIGLU_ALPHA = 1.702
MOE_BLOCK = 256

LANES = 128
VMEM_LIMIT = 56 * 1024 * 1024


def _cparams(*sem):
    return pltpu.CompilerParams(dimension_semantics=sem, vmem_limit_bytes=VMEM_LIMIT)


def _chunk_tile(lp, target):
    best = CHUNK
    for t in range(CHUNK, target + 1, CHUNK):
        if lp % t == 0:
            best = t
    return best


def _dot(a, b):
    return jnp.dot(a, b, preferred_element_type=F32)


def _dot_nt(a, b):
    return lax.dot_general(a, b, (((1,), (1,)), ((), ())), preferred_element_type=F32)


def _dot_tn(a, b):
    return lax.dot_general(a, b, (((0,), (0,)), ((), ())), preferred_element_type=F32)


def _split3(x):
    hi = x.astype(BF16)
    r1 = x - hi.astype(F32)
    mid = r1.astype(BF16)
    lo = (r1 - mid.astype(F32)).astype(BF16)
    return hi, mid, lo


def _dot01_left(t01, x):
    hi, mid, lo = _split3(x)
    return _dot(t01, hi) + _dot(t01, mid) + _dot(t01, lo)


def _dot01_right(x, t01):
    hi, mid, lo = _split3(x)
    return _dot(hi, t01) + _dot(mid, t01) + _dot(lo, t01)


def _row_sum_lanes(x):
    ones = jnp.ones((x.shape[1], LANES), BF16)
    hi = x.astype(BF16)
    lo = (x - hi.astype(F32)).astype(BF16)
    return _dot(hi, ones) + _dot(lo, ones)


def _sigmoid(x):
    return 1.0 / (1.0 + jnp.exp(-x))


def _softplus(x):
    return jnp.maximum(x, 0.0) + jnp.log(1.0 + jnp.exp(-jnp.abs(x)))


def _log_sigmoid(x):
    return -_softplus(-x)


def _silu(x):
    return x * _sigmoid(x)


def _gelu_tanh(x):
    c = math.sqrt(2.0 / math.pi)
    return 0.5 * x * (1.0 + jnp.tanh(c * (x + 0.044715 * (x * x * x))))


def _iota(shape, dim):
    return lax.broadcasted_iota(jnp.int32, shape, dim)


def _tri_masks(n):
    r = _iota((n, n), 0)
    c = _iota((n, n), 1)
    return r, c


def _proj_kernel(x_ref, w_ref, wg_ref, wgt_ref, y_ref, gcol_ref, grow_ref):
    xb = x_ref[...].astype(BF16)
    y_ref[...] = _dot(xb, w_ref[...])

    @pl.when(pl.program_id(1) == 0)
    def _():
        gcol_ref[...] = _dot(xb, wg_ref[...])
        grow = _dot_nt(wgt_ref[...], xb)
        for j in range(grow_ref.shape[0]):
            grow_ref[j] = grow[:, j * CHUNK:(j + 1) * CHUNK]


def _in_proj(h, w_main, w_gate, w_gate_t, tm, tn):
    r, d = h.shape
    n = w_main.shape[1]
    cpt = tm // CHUNK
    return pl.pallas_call(
        _proj_kernel,
        grid=(r // tm, n // tn),
        in_specs=[pl.BlockSpec((tm, d), lambda i, j: (i, 0)),
                  pl.BlockSpec((d, tn), lambda i, j: (0, j)),
                  pl.BlockSpec((d, LANES), lambda i, j: (0, 0)),
                  pl.BlockSpec((16, d), lambda i, j: (0, 0))],
        out_specs=[pl.BlockSpec((tm, tn), lambda i, j: (i, j)),
                   pl.BlockSpec((tm, LANES), lambda i, j: (i, 0)),
                   pl.BlockSpec((cpt, 16, CHUNK), lambda i, j: (i, 0, 0))],
        out_shape=[jax.ShapeDtypeStruct((r, n), F32),
                   jax.ShapeDtypeStruct((r, LANES), F32),
                   jax.ShapeDtypeStruct((r // CHUNK, 16, CHUNK), F32)],
        compiler_params=_cparams("parallel", "arbitrary"),
        name="in_proj",
    )(h, w_main, w_gate, w_gate_t)


def _mlstm_kernel(q_ref, k_ref, v_ref, o_ref, gcol_ref, grow_ref, bcol_ref, brow_ref, hn_ref,
                  out_ref, c_scr, m_scr):
    c = pl.program_id(1)

    @pl.when(c == 0)
    def _():
        c_scr[...] = jnp.zeros_like(c_scr)
        m_scr[...] = jnp.zeros_like(m_scr)

    n = CHUNK
    ri, ci = _tri_masks(n)
    causal = ri >= ci
    tri_l = jnp.where(causal, 1.0, 0.0).astype(BF16)
    tri_u = jnp.where(ri <= ci, 1.0, 0.0).astype(BF16)
    first = c == 0
    pad_c = jnp.logical_and(first, _iota((n, 1), 0) < PAD)
    pad_r = jnp.logical_and(first, _iota((1, n), 1) < PAD)

    i_c, b_c, i_r, b_r = [], [], [], []
    for sq in range(q_ref.shape[0]):
        gc = A_GATE_CAP * jnp.tanh((gcol_ref[sq] + bcol_ref[...]) * (1.0 / A_GATE_CAP))
        i_c.append(jnp.where(pad_c, NEG, gc))
        b_c.append(_dot01_left(tri_l, jnp.where(pad_c, 0.0, _log_sigmoid(gc))))
        gr = A_GATE_CAP * jnp.tanh((grow_ref[sq] + brow_ref[...]) * (1.0 / A_GATE_CAP))
        i_r.append(jnp.where(pad_r, NEG, gr))
        b_r.append(_dot01_right(jnp.where(pad_r, 0.0, _log_sigmoid(gr)), tri_u))

    one_col = jnp.where(_iota((n, A_DV), 1) == 0, 1.0, 0.0).astype(BF16)
    chains = [(sq, hd) for sq in range(q_ref.shape[0]) for hd in range(A_HEADS)]
    heads = range(len(chains))
    sl = [slice(hd * LANES, (hd + 1) * LANES) for _, hd in chains]
    q = [q_ref[sq, :, sl[h]].astype(BF16) for h, (sq, _) in enumerate(chains)]
    kf = [k_ref[sq, :, sl[h]] * (A_DQK ** -0.5) for h, (sq, _) in enumerate(chains)]
    vext = [jnp.concatenate([v_ref[sq, :, sl[h]].astype(BF16), one_col], axis=1)
            for h, (sq, _) in enumerate(chains)]
    bi = [b_c[sq][:, A_HEADS + hd:A_HEADS + hd + 1] for sq, hd in chains]
    ii = [i_c[sq][:, hd:hd + 1] for sq, hd in chains]
    bj = [b_r[sq][A_HEADS + hd:A_HEADS + hd + 1, :] for sq, hd in chains]
    ij = [i_r[sq][hd:hd + 1, :] for sq, hd in chains]
    b_last = [t[n - 1:n, :] for t in bi]
    a_end = [b_last[h] - bi[h] + ii[h] for h in heads]
    m_loc = [jnp.max(t, axis=0, keepdims=True) for t in a_end]
    m0 = [m_scr[h][0:1, 0:1] for h in heads]
    d_intra = [jnp.where(causal, bi[h] - bj[h] + ij[h], NEG) for h in heads]
    d_inter = [bi[h] + m0[h] for h in heads]
    m_row = [jnp.maximum(jnp.max(d_intra[h], axis=1, keepdims=True), d_inter[h]) for h in heads]
    s = [jnp.exp(d_inter[h] - m_row[h]) for h in heads]
    qk = [(_dot_nt(q[h], kf[h].astype(BF16)) * jnp.exp(d_intra[h] - m_row[h])).astype(BF16) for h in heads]
    cext = [c_scr[h] for h in heads]
    num_ext = [_dot(qk[h], vext[h]) + s[h] * _dot(q[h], cext[h].astype(BF16)) for h in heads]
    w_end = [jnp.exp(a_end[h] - m_loc[h]) for h in heads]
    d_ext = [_dot_tn((kf[h] * w_end[h]).astype(BF16), vext[h]) for h in heads]
    for h in heads:
        m_new = jnp.maximum(b_last[h] + m0[h], m_loc[h])
        c_scr[h] = jnp.exp(b_last[h] + m0[h] - m_new) * cext[h] + jnp.exp(m_loc[h] - m_new) * d_ext[h]
        m_scr[h] = jnp.broadcast_to(m_new, (8, LANES))
    hh = [num_ext[h][:, :A_DV] / jnp.maximum(jnp.abs(num_ext[h][:, A_DV:A_DV + 1]), jnp.exp(-m_row[h]))
          for h in heads]
    mu = [_row_sum_lanes(t) * (1.0 / A_DV) for t in hh]
    hc = [hh[h] - mu[h] for h in heads]
    var = [_row_sum_lanes(t * t) * (1.0 / A_DV) for t in hc]
    for h, (sq, _) in enumerate(chains):
        hnorm = hc[h] * lax.rsqrt(var[h] + 1e-6) * hn_ref[:, sl[h]]
        out_ref[sq, :, sl[h]] = _sigmoid(o_ref[sq, :, sl[h]]) * hnorm


MLSTM_SEQS_PER_STEP = 2


def _mlstm(y, gcol, grow3, bias_col, bias_row, head_norm, bsz, nc):
    r = y.shape[0]
    lp = r // bsz
    w = A_HEADS * LANES
    ns = MLSTM_SEQS_PER_STEP if bsz % MLSTM_SEQS_PER_STEP == 0 else 1
    y4 = y.reshape(bsz // ns, ns, lp, y.shape[1])
    col = lambda j: pl.BlockSpec((None, ns, CHUNK, w), lambda b, c: (b, 0, c, j))
    out = pl.pallas_call(
        _mlstm_kernel,
        grid=(bsz // ns, nc),
        in_specs=[col(0), col(1), col(2), col(3),
                  pl.BlockSpec((None, ns, CHUNK, LANES), lambda b, c: (b, 0, c, 0)),
                  pl.BlockSpec((None, ns, None, 16, CHUNK), lambda b, c: (b, 0, c, 0, 0)),
                  pl.BlockSpec((1, LANES), lambda b, c: (0, 0)),
                  pl.BlockSpec((16, CHUNK), lambda b, c: (0, 0)),
                  pl.BlockSpec((1, w), lambda b, c: (0, 0))],
        out_specs=pl.BlockSpec((None, ns, CHUNK, w), lambda b, c: (b, 0, c, 0)),
        out_shape=jax.ShapeDtypeStruct((bsz // ns, ns, lp, w), F32),
        scratch_shapes=[pltpu.VMEM((ns * A_HEADS, LANES, 2 * LANES), F32),
                        pltpu.VMEM((ns * A_HEADS, 8, LANES), F32)],
        compiler_params=_cparams("parallel", "arbitrary"),
        name="mlstm",
    )(y4, y4, y4, y4, gcol.reshape(bsz // ns, ns, lp, LANES), grow3.reshape(bsz // ns, ns, nc, 16, CHUNK),
      bias_col, bias_row, head_norm)
    return out.reshape(r, w)


def _s5_params(a_re, a_im, log_step, b_re, b_im, c_re, c_im, d_skip, levels):
    hp = lax.Precision.HIGHEST
    g, p = a_re.shape
    dt = jnp.exp(log_step)[:, None]
    mag = jnp.exp(a_re * dt)
    lb_re, lb_im = mag * jnp.cos(a_im * dt), mag * jnp.sin(a_im * dt)
    inv = 1.0 / (a_re * a_re + a_im * a_im)
    zr, zi = lb_re - 1.0, lb_im
    fr = (zr * a_re + zi * a_im) * inv
    fi = (zi * a_re - zr * a_im) * inv
    bb_re = fr[..., None] * b_re - fi[..., None] * b_im
    bb_im = fr[..., None] * b_im + fi[..., None] * b_re
    n = S5_CHUNK
    tau = jnp.arange(n + 1, dtype=F32)[:, None, None]
    pm = jnp.exp(tau * (a_re * dt))
    pr, pi = pm * jnp.cos(tau * (a_im * dt)), pm * jnp.sin(tau * (a_im * dt))
    e_re = pr[..., None] * bb_re - pi[..., None] * bb_im
    e_im = pr[..., None] * bb_im + pi[..., None] * bb_re
    kern = (jnp.einsum('gcp,tgpd->tgcd', c_re, e_re[:n], precision=hp)
            - jnp.einsum('gcp,tgpd->tgcd', c_im, e_im[:n], precision=hp))
    idx = jnp.arange(n)
    diff = idx[None, :] - idx[:, None]
    kd = jnp.where((diff >= 0)[:, :, None, None, None], kern[jnp.clip(diff, 0, n - 1)], 0.0)
    m_t = kd.transpose(2, 0, 4, 1, 3).reshape(g, n * B_GROUP, n * B_GROUP)
    w_re = e_re[n - 1 - idx].transpose(1, 0, 3, 2)
    w_im = e_im[n - 1 - idx].transpose(1, 0, 3, 2)
    w_t = jnp.concatenate([w_re, w_im], axis=-1).reshape(g, n * B_GROUP, 2 * p)
    f_re = c_re[None] * pr[1:, :, None, :] - c_im[None] * pi[1:, :, None, :]
    f_im = c_re[None] * pi[1:, :, None, :] + c_im[None] * pr[1:, :, None, :]
    v_t = jnp.concatenate([f_re.transpose(1, 3, 0, 2), -f_im.transpose(1, 3, 0, 2)], axis=1)
    v_t = v_t.reshape(g, 2 * p, n * B_GROUP)
    steps = (n * 2.0 ** jnp.arange(levels, dtype=F32))[:, None, None]
    sm = jnp.exp(steps * (a_re * dt))
    sr, si = sm * jnp.cos(steps * (a_im * dt)), sm * jnp.sin(steps * (a_im * dt))
    lam_a = jnp.concatenate([sr, sr], axis=-1)[:, :, None, :]
    lam_b = jnp.concatenate([-si, si], axis=-1)[:, :, None, :]
    d_flat = jnp.tile(d_skip, (1, n))[:, None, :]
    return m_t.astype(BF16), w_t.astype(BF16), v_t.astype(BF16), lam_a, lam_b, d_flat


S5_GROUPS_PER_STEP = LANES // B_GROUP
S5_SCAN_OFF = 128


def _s5_kernel(u_ref, mt_ref, wt_ref, vt_ref, d_ref, la_ref, lb_ref, out_ref, scan_scr, *, nch):
    n = S5_CHUNK
    gps = S5_GROUPS_PER_STEP
    rows = 2 * nch
    levels = la_ref.shape[0]
    lane_blk = _iota((rows, LANES), 1) // B_GROUP
    in_blk = [lane_blk == i for i in range(gps)]
    r = _iota((rows, 1), 0)
    r_in = jnp.where(r >= nch, r - nch, r)
    xs = [u_ref[pl.ds(s, rows, stride=n), :] for s in range(n)]

    def shifted(x, lanes):
        return pltpu.roll(x, lanes % LANES, 1) if lanes % LANES else x

    ys = []
    for g in range(gps):
        halves = []
        for half in range(2):
            acc = jnp.zeros((rows, LANES), F32)
            for sp in range(gps):
                acc = jnp.where(in_blk[sp], shifted(xs[half * gps + sp], B_GROUP * (sp - g)), acc)
            halves.append(acc)
        u = jnp.concatenate(halves, axis=1)
        ub = u.astype(BF16)
        y = _dot(ub, mt_ref[g]) + d_ref[g] * u
        x = _dot(ub, wt_ref[g])
        scan_scr[g, 0:S5_SCAN_OFF, :] = jnp.zeros((S5_SCAN_OFF, LANES), F32)
        for lv in range(levels):
            sh = 1 << lv
            scan_scr[g, S5_SCAN_OFF:S5_SCAN_OFF + rows, :] = x
            prev = scan_scr[g, S5_SCAN_OFF - sh:S5_SCAN_OFF - sh + rows, :]
            prev = jnp.where(r_in >= sh, prev, 0.0)
            x = x + la_ref[lv, g] * prev + lb_ref[lv, g] * pltpu.roll(prev, B_STATE, 1)
        scan_scr[g, S5_SCAN_OFF:S5_SCAN_OFF + rows, :] = x
        x0 = scan_scr[g, S5_SCAN_OFF - 1:S5_SCAN_OFF - 1 + rows, :]
        x0 = jnp.where(r_in >= 1, x0, 0.0)
        ys.append(y + _dot(x0.astype(BF16), vt_ref[g]))
    for s in range(n):
        half, sp = divmod(s, gps)
        acc = jnp.zeros((rows, LANES), F32)
        for g in range(gps):
            acc = jnp.where(in_blk[g], shifted(ys[g][:, half * LANES:(half + 1) * LANES], B_GROUP * (g - sp)), acc)
        out_ref[pl.ds(s, rows, stride=n), :] = acc


def _s5(y, u_col0, params, bsz, lp):
    m_t, w_t, v_t, lam_a, lam_b, d_flat = params
    r = y.shape[0]
    nch = lp // S5_CHUNK
    gps = S5_GROUPS_PER_STEP
    wide = S5_CHUNK * B_GROUP
    st = 2 * B_STATE
    levels = lam_a.shape[0]
    assert bsz % 2 == 0 and (1 << (levels - 1)) <= S5_SCAN_OFF and u_col0 % LANES == 0
    gspec = lambda *shape: pl.BlockSpec((gps,) + shape, lambda j, b: (j, 0, 0))
    lspec = pl.BlockSpec((levels, gps, 1, st), lambda j, b: (0, j, 0, 0))
    return pl.pallas_call(
        functools.partial(_s5_kernel, nch=nch),
        grid=(B_GROUPS // gps, bsz // 2),
        in_specs=[pl.BlockSpec((2 * lp, LANES), lambda j, b: (b, u_col0 // LANES + j)),
                  gspec(wide, wide), gspec(wide, st), gspec(st, wide), gspec(1, wide), lspec, lspec],
        out_specs=pl.BlockSpec((2 * lp, LANES), lambda j, b: (b, j)),
        out_shape=jax.ShapeDtypeStruct((r, B_CH), F32),
        scratch_shapes=[pltpu.VMEM((gps, S5_SCAN_OFF + 2 * nch, LANES), F32)],
        compiler_params=_cparams("parallel", "arbitrary"),
        name="s5",
    )(y, m_t, w_t, v_t, d_flat, lam_a, lam_b)


def _res_ln(h, mix, g, b, pad_rows):
    z = DN_ALPHA * h + mix
    mu = jnp.mean(z, axis=1, keepdims=True)
    zc = z - mu
    var = jnp.mean(zc * zc, axis=1, keepdims=True)
    out = zc * lax.rsqrt(var + LN_EPS) * g + b
    return jnp.where(pad_rows, 0.0, out)


def _pad_rows_mask(tm, lp):
    start = (pl.program_id(0) * tm) % lp
    pos = start + _iota((tm, 1), 0)
    pos = jnp.where(pos >= lp, pos - lp, pos)
    return pos < PAD


def _even_out_kernel(ha_ref, ys_ref, h_ref, wglu_ref, bglu_ref, wout_ref, g_ref, b_ref, out_ref,
                     *, lp):
    tm = h_ref.shape[0]
    yb = _gelu_tanh(ys_ref[...])
    hb = yb * _sigmoid(_dot(yb.astype(BF16), wglu_ref[...]) + bglu_ref[...])
    av = ha_ref.shape[1]
    mix = _dot(ha_ref[...].astype(BF16), wout_ref[:av, :]) + _dot(hb.astype(BF16), wout_ref[av:, :])
    out_ref[...] = _res_ln(h_ref[...], mix, g_ref[...], b_ref[...], _pad_rows_mask(tm, lp))


def _even_out(ha, ys, h, w_glu, b_glu, w_out, ln_g, ln_b, tm, lp):
    r, d = h.shape
    av, bc = ha.shape[1], ys.shape[1]
    full = lambda shape: pl.BlockSpec(shape, lambda i: (0, 0))
    return pl.pallas_call(
        functools.partial(_even_out_kernel, lp=lp),
        grid=(r // tm,),
        in_specs=[pl.BlockSpec((tm, av), lambda i: (i, 0)),
                  pl.BlockSpec((tm, bc), lambda i: (i, 0)),
                  pl.BlockSpec((tm, d), lambda i: (i, 0)),
                  full((bc, bc)), full((1, bc)), full((av + bc, d)), full((1, d)), full((1, d))],
        out_specs=pl.BlockSpec((tm, d), lambda i: (i, 0)),
        out_shape=jax.ShapeDtypeStruct((r, d), F32),
        compiler_params=_cparams("parallel"),
        name="even_out",
    )(ha, ys, h, w_glu, b_glu, w_out, ln_g, ln_b)


def _gdn_kernel(q_ref, k_ref, v_ref, z_ref, gcol_ref, grow_ref, conv_ref, pcol_ref, prow_ref,
                nw_ref, out_ref, s_scr, carry_scr, ext_scr):
    c = pl.program_id(1)

    @pl.when(c == 0)
    def _():
        s_scr[...] = jnp.zeros_like(s_scr)
        carry_scr[...] = jnp.zeros_like(carry_scr)

    n = CHUNK
    ri, ci = _tri_masks(n)
    incl = ri >= ci
    strict = ri > ci
    tri_l = jnp.where(incl, 1.0, 0.0).astype(BF16)
    tri_u = jnp.where(ri <= ci, 1.0, 0.0).astype(BF16)
    first = c == 0
    pad_c = jnp.logical_and(first, _iota((n, 1), 0) < PAD)
    pad_r = jnp.logical_and(first, _iota((1, n), 1) < PAD)

    gc = gcol_ref[...]
    beta_c = jnp.where(pad_c, 0.0, _sigmoid(gc))
    g_c = jnp.where(pad_c, 0.0, pcol_ref[1:2, :] * _softplus(gc + pcol_ref[0:1, :]))
    gcum_c = _dot01_left(tri_l, g_c)
    gr = grow_ref[0]
    g_r = jnp.where(pad_r, 0.0, prow_ref[1] * _softplus(gr + prow_ref[0]))
    gcum_r = _dot01_right(g_r, tri_u)

    width = q_ref.shape[1]
    ext_scr[0:8, :] = carry_scr[...]
    for j, ref in enumerate((q_ref, k_ref, v_ref)):
        ext_scr[8:8 + n, j * width:(j + 1) * width] = ref[...]
    carry_scr[...] = ext_scr[n:n + 8, :]

    def conv(col):
        acc = conv_ref[0:1, col] * ext_scr[5:5 + n, col]
        for j in range(1, C_CONV):
            acc = acc + conv_ref[j:j + 1, col] * ext_scr[5 + j:5 + j + n, col]
        return _silu(acc)

    def l2n(x):
        return x * lax.rsqrt(jnp.sum(x * x, axis=1, keepdims=True) + 1e-6)

    heads = range(C_HEADS)
    lanes = lambda base, h: slice(base + h * LANES, base + (h + 1) * LANES)
    q = [l2n(conv(lanes(0, h))) * (C_DK ** -0.5) for h in heads]
    k = [l2n(conv(lanes(width, h))) for h in heads]
    v = [conv(lanes(2 * width, h)) for h in heads]
    beta = [beta_c[:, h:h + 1] for h in heads]
    gi = [gcum_c[:, C_HEADS + h:C_HEADS + h + 1] for h in heads]
    gj = [gcum_r[C_HEADS + h:C_HEADS + h + 1, :] for h in heads]
    g_last = [g[n - 1:n, :] for g in gi]
    decay = [jnp.where(incl, jnp.exp(jnp.where(incl, gi[h] - gj[h], 0.0)), 0.0) for h in heads]
    eg = [jnp.exp(g) for g in gi]
    kb = [t.astype(BF16) for t in k]
    qkk = [_dot_nt(jnp.concatenate([q[h].astype(BF16), kb[h]], axis=0), kb[h]) for h in heads]
    attn = [(qkk[h][:n] * decay[h]).astype(BF16) for h in heads]
    ab = [jnp.where(strict, -(beta[h] * qkk[h][n:] * decay[h]), 0.0).astype(BF16) for h in heads]
    x = [jnp.concatenate([beta[h] * v[h], (beta[h] * eg[h]) * k[h]], axis=1) for h in heads]
    x = [x[h] + _dot(ab[h], x[h].astype(BF16)) for h in heads]
    for _ in range(5):
        ab = [_dot(t, t).astype(BF16) for t in ab]
        x = [x[h] + _dot(ab[h], x[h].astype(BF16)) for h in heads]
    s0 = [s_scr[h] for h in heads]
    q_dec = [(q[h] * eg[h]).astype(BF16) for h in heads]
    k_dec = [(k[h] * jnp.exp(g_last[h] - gi[h])).astype(BF16) for h in heads]
    ws_qs = [_dot(jnp.concatenate([x[h][:, LANES:].astype(BF16), q_dec[h]], axis=0), s0[h].astype(BF16))
             for h in heads]
    vb = [(x[h][:, :LANES] - ws_qs[h][:n]).astype(BF16) for h in heads]
    o = [ws_qs[h][n:] + _dot(attn[h], vb[h]) for h in heads]
    for h in heads:
        s_scr[h] = jnp.exp(g_last[h]) * s0[h] + _dot_tn(k_dec[h], vb[h])
    ms = [jnp.mean(t * t, axis=1, keepdims=True) for t in o]
    for h in heads:
        on = o[h] * lax.rsqrt(ms[h] + 1e-6) * nw_ref[...]
        out_ref[:, lanes(0, h)] = on * _silu(z_ref[:, lanes(0, h)])


def _gdn(y, gcol, grow3, conv_w, pcol, prow, norm_w, bsz, nc):
    r = y.shape[0]
    w = C_HEADS * C_DK
    row = lambda b, c: b * nc + c
    return pl.pallas_call(
        _gdn_kernel,
        grid=(bsz, nc),
        in_specs=[pl.BlockSpec((CHUNK, w), lambda b, c: (row(b, c), 0)),
                  pl.BlockSpec((CHUNK, w), lambda b, c: (row(b, c), 1)),
                  pl.BlockSpec((CHUNK, w), lambda b, c: (row(b, c), 2)),
                  pl.BlockSpec((CHUNK, w), lambda b, c: (row(b, c), 3)),
                  pl.BlockSpec((CHUNK, LANES), lambda b, c: (row(b, c), 0)),
                  pl.BlockSpec((1, 16, CHUNK), lambda b, c: (row(b, c), 0, 0)),
                  pl.BlockSpec((C_CONV, 3 * w), lambda b, c: (0, 0)),
                  pl.BlockSpec((2, LANES), lambda b, c: (0, 0)),
                  pl.BlockSpec((2, 16, CHUNK), lambda b, c: (0, 0, 0)),
                  pl.BlockSpec((1, LANES), lambda b, c: (0, 0))],
        out_specs=pl.BlockSpec((CHUNK, w), lambda b, c: (row(b, c), 0)),
        out_shape=jax.ShapeDtypeStruct((r, w), F32),
        scratch_shapes=[pltpu.VMEM((C_HEADS, C_DK, C_DK), F32),
                        pltpu.VMEM((8, 3 * w), F32),
                        pltpu.VMEM((CHUNK + 8, 3 * w), F32)],
        compiler_params=_cparams("parallel", "arbitrary"),
        name="gdn",
    )(y, y, y, y, gcol, grow3, conv_w, pcol, prow, norm_w)


def _odd_out_kernel(o_ref, h_ref, wout_ref, g_ref, b_ref, out_ref, *, lp):
    tm = h_ref.shape[0]
    mix = _dot(o_ref[...].astype(BF16), wout_ref[...])
    out_ref[...] = _res_ln(h_ref[...], mix, g_ref[...], b_ref[...], _pad_rows_mask(tm, lp))


def _odd_out(o, h, w_out, ln_g, ln_b, tm, lp):
    r, d = h.shape
    full = lambda shape: pl.BlockSpec(shape, lambda i: (0, 0))
    return pl.pallas_call(
        functools.partial(_odd_out_kernel, lp=lp),
        grid=(r // tm,),
        in_specs=[pl.BlockSpec((tm, d), lambda i: (i, 0)),
                  pl.BlockSpec((tm, d), lambda i: (i, 0)),
                  full((d, d)), full((1, d)), full((1, d))],
        out_specs=pl.BlockSpec((tm, d), lambda i: (i, 0)),
        out_shape=jax.ShapeDtypeStruct((r, d), F32),
        compiler_params=_cparams("parallel"),
        name="odd_out",
    )(o, h, w_out, ln_g, ln_b)


def _router_kernel(x_ref, w_ref, b_ref, meta_ref, gate_ref, cnt_ref, base_scr):
    @pl.when(pl.program_id(0) == 0)
    def _():
        base_scr[...] = jnp.zeros_like(base_scr)

    tm = x_ref.shape[0]
    logits = _dot(x_ref[...].astype(BF16), w_ref[...]) + b_ref[...]
    lane = _iota((tm, LANES), 1)
    lane_f = lane.astype(F32)
    work = logits
    vals, sels = [], []
    onehot = jnp.zeros((tm, LANES), F32)
    for _ in range(TOP_K):
        m = jnp.max(work, axis=1, keepdims=True)
        idx = jnp.min(jnp.where(work == m, lane_f, float(LANES)), axis=1, keepdims=True)
        sel = lane_f == idx
        vals.append(m)
        sels.append((sel, idx))
        onehot = onehot + jnp.where(sel, 1.0, 0.0)
        work = jnp.where(sel, -jnp.inf, work)
    ri, ci = _tri_masks(tm)
    tri = jnp.where(ri > ci, 1.0, 0.0).astype(BF16)
    before = _dot(tri, onehot.astype(BF16)) + base_scr[0:1, :]
    base_scr[...] = base_scr[...] + jnp.sum(onehot, axis=0, keepdims=True)
    cnt_ref[...] = base_scr[...]
    es = [jnp.exp(v - vals[0]) for v in vals]
    tot = es[0] + es[1] + es[2] + es[3]
    meta = jnp.zeros((tm, LANES), jnp.int32)
    gate = jnp.zeros((tm, LANES), F32)
    for k, (sel, idx) in enumerate(sels):
        rank = jnp.sum(jnp.where(sel, before, 0.0), axis=1, keepdims=True)
        meta = jnp.where(lane == k, idx.astype(jnp.int32), meta)
        meta = jnp.where(lane == TOP_K + k, rank.astype(jnp.int32), meta)
        gate = jnp.where(lane == k, es[k] / tot, gate)
    meta_ref[...] = meta
    gate_ref[...] = gate


def _router(h, w_r, b_r, tm):
    r, d = h.shape
    return pl.pallas_call(
        _router_kernel,
        grid=(r // tm,),
        in_specs=[pl.BlockSpec((tm, d), lambda i: (i, 0)),
                  pl.BlockSpec((d, LANES), lambda i: (0, 0)),
                  pl.BlockSpec((1, LANES), lambda i: (0, 0))],
        out_specs=[pl.BlockSpec((tm, LANES), lambda i: (i, 0)),
                   pl.BlockSpec((tm, LANES), lambda i: (i, 0)),
                   pl.BlockSpec((8, LANES), lambda i: (0, 0))],
        out_shape=[jax.ShapeDtypeStruct((r, LANES), jnp.int32),
                   jax.ShapeDtypeStruct((r, LANES), F32),
                   jax.ShapeDtypeStruct((8, LANES), F32)],
        scratch_shapes=[pltpu.VMEM((8, LANES), F32)],
        compiler_params=_cparams("arbitrary"),
        name="router",
    )(h, w_r, b_r)


ROW_DMA_UNROLL = 8
ROW_DMA_WAIT_GROUP = 64


def _issue_row_copies(make_copy, tm):
    def issue(i, carry):
        for u in range(ROW_DMA_UNROLL):
            for k in range(TOP_K):
                make_copy(i * ROW_DMA_UNROLL + u, k).start(priority=(u * TOP_K + k) % 2)
        return carry

    lax.fori_loop(0, tm // ROW_DMA_UNROLL, issue, 0)


def _wait_row_copies(make_copy, tm):
    def drain(i, carry):
        for _ in range(ROW_DMA_WAIT_GROUP):
            make_copy(0, 0).wait()
        return carry

    lax.fori_loop(0, tm * TOP_K // ROW_DMA_WAIT_GROUP, drain, 0)


def _dispatch_kernel(dest_ref, x_ref, xs_in, xs_hbm, sem, *, tm):
    del xs_in

    def make_copy(t, k):
        return pltpu.make_async_copy(x_ref.at[pl.ds(t, 1)],
                                     xs_hbm.at[pl.ds(dest_ref[t * TOP_K + k], 1)], sem)

    _issue_row_copies(make_copy, tm)
    _wait_row_copies(make_copy, tm)


def _dispatch(h, dest_flat, xs0, tm):
    r, d = h.shape
    n_slots = xs0.shape[0]
    return pl.pallas_call(
        functools.partial(_dispatch_kernel, tm=tm),
        grid=(r // tm,),
        in_specs=[pl.BlockSpec((tm * TOP_K,), lambda i: (i,), memory_space=pltpu.SMEM),
                  pl.BlockSpec((tm, d), lambda i: (i, 0)),
                  pl.BlockSpec(memory_space=pl.ANY)],
        out_specs=pl.BlockSpec(memory_space=pl.ANY),
        out_shape=jax.ShapeDtypeStruct((n_slots, d), h.dtype),
        scratch_shapes=[pltpu.SemaphoreType.DMA],
        input_output_aliases={2: 0},
        compiler_params=_cparams("arbitrary"),
        name="moe_dispatch",
    )(dest_flat, h, xs0)


def _expert_kernel(be_ref, nu_ref, x_ref, wgu_ref, bgu_ref, wd_ref, bd_ref, out_ref, wgu_scr, wd_scr):
    b = pl.program_id(0)
    prev = be_ref[jnp.maximum(b - 1, 0)]
    fresh = jnp.logical_or(b == 0, be_ref[b] != prev)
    tile = 2 * LANES
    n_tiles = wgu_ref.shape[2] // tile

    @pl.when(fresh)
    def _():
        r = _iota((tile, tile), 0)
        c = _iota((tile, tile), 1)
        src = jnp.where(c < LANES, 2 * c, 2 * (c - LANES) + 1)
        perm = jnp.where(r == src, 1.0, 0.0).astype(BF16)
        for j in range(n_tiles):
            cols = slice(j * tile, (j + 1) * tile)
            wgu_scr[:, cols] = _dot(wgu_ref[0, :, cols].astype(BF16), perm).astype(BF16)
        wd_scr[...] = wd_ref[0].astype(BF16)

    @pl.when(b < nu_ref[0])
    def _():
        xb = x_ref[...].astype(BF16)
        h = _dot(xb, wgu_scr[...]) + bgu_ref[0]
        acts = []
        for j in range(n_tiles):
            gate = jnp.minimum(h[:, j * tile:j * tile + LANES], SWIGLU_LIMIT)
            up = jnp.clip(h[:, j * tile + LANES:(j + 1) * tile], -SWIGLU_LIMIT, SWIGLU_LIMIT)
            acts.append(((up + 1.0) * gate * _sigmoid(SWIGLU_ALPHA * gate)).astype(BF16))
        act = jnp.concatenate(acts, axis=1)
        out_ref[...] = _dot(act, wd_scr[...]) + bd_ref[0]

    @pl.when(b >= nu_ref[0])
    def _():
        out_ref[...] = jnp.zeros_like(out_ref)


def _experts(xs, block_e, n_used, w_gu, b_gu, w_d, b_d):
    n_slots, d = xs.shape
    n_blocks = n_slots // MOE_BLOCK
    de2 = w_gu.shape[2]
    de = w_d.shape[1]
    grid_spec = pltpu.PrefetchScalarGridSpec(
        num_scalar_prefetch=2,
        grid=(n_blocks,),
        in_specs=[pl.BlockSpec((MOE_BLOCK, d), lambda b, be, nu: (b, 0)),
                  pl.BlockSpec((1, d, de2), lambda b, be, nu: (be[b], 0, 0)),
                  pl.BlockSpec((1, 1, de2), lambda b, be, nu: (be[b], 0, 0)),
                  pl.BlockSpec((1, de, d), lambda b, be, nu: (be[b], 0, 0)),
                  pl.BlockSpec((1, 1, d), lambda b, be, nu: (be[b], 0, 0))],
        out_specs=pl.BlockSpec((MOE_BLOCK, d), lambda b, be, nu: (b, 0)),
        scratch_shapes=[pltpu.VMEM((d, de2), BF16), pltpu.VMEM((de, d), BF16)],
    )
    return pl.pallas_call(
        _expert_kernel,
        grid_spec=grid_spec,
        out_shape=jax.ShapeDtypeStruct((n_slots, d), F32),
        compiler_params=_cparams("arbitrary"),
        name="moe_experts",
    )(block_e, n_used, xs, w_gu, b_gu, w_d, b_d)


def _combine_kernel(dest_ref, dest_next_ref, gate_ref, h_ref, g_ref, b_ref, eo_hbm, out_ref, buf, sems,
                    *, lp):
    tm = h_ref.shape[0]
    i = pl.program_id(0)
    slot = i % 2

    def gather(idx_ref, into):
        def make_copy(t, k):
            return pltpu.make_async_copy(eo_hbm.at[pl.ds(idx_ref[t * TOP_K + k], 1)],
                                         buf.at[into, k, pl.ds(t, 1)], sems.at[into])
        return make_copy

    @pl.when(i == 0)
    def _():
        _issue_row_copies(gather(dest_ref, slot), tm)

    @pl.when(i + 1 < pl.num_programs(0))
    def _():
        _issue_row_copies(gather(dest_next_ref, 1 - slot), tm)

    _wait_row_copies(gather(dest_ref, slot), tm)

    gates = gate_ref[...]
    ffn = gates[:, 0:1] * buf[slot, 0]
    for k in range(1, TOP_K):
        ffn = ffn + gates[:, k:k + 1] * buf[slot, k]
    out_ref[...] = _res_ln(h_ref[...], ffn, g_ref[...], b_ref[...], _pad_rows_mask(tm, lp))


def _combine(eo, dest_flat, gates, h, ln_g, ln_b, tm, lp):
    r, d = h.shape
    n_tiles = r // tm
    full = lambda shape: pl.BlockSpec(shape, lambda i: (0, 0))
    return pl.pallas_call(
        functools.partial(_combine_kernel, lp=lp),
        grid=(n_tiles,),
        in_specs=[pl.BlockSpec((tm * TOP_K,), lambda i: (i,), memory_space=pltpu.SMEM),
                  pl.BlockSpec((tm * TOP_K,), lambda i: (jnp.minimum(i + 1, n_tiles - 1),),
                               memory_space=pltpu.SMEM),
                  pl.BlockSpec((tm, LANES), lambda i: (i, 0)),
                  pl.BlockSpec((tm, d), lambda i: (i, 0)),
                  full((1, d)), full((1, d)),
                  pl.BlockSpec(memory_space=pl.ANY)],
        out_specs=pl.BlockSpec((tm, d), lambda i: (i, 0)),
        out_shape=jax.ShapeDtypeStruct((r, d), F32),
        scratch_shapes=[pltpu.VMEM((2, TOP_K, tm, d), F32), pltpu.SemaphoreType.DMA((2,))],
        compiler_params=_cparams("arbitrary"),
        name="moe_combine",
    )(dest_flat, dest_flat, gates, h, ln_g, ln_b, eo)


def _moe(h, w_router, b_router, w_gu, b_gu, w_d, b_d, ln_g, ln_b, tm, lp, xs_prev):
    r, d = h.shape
    w_r = jnp.pad(w_router, ((0, 0), (0, LANES - N_EXPERTS))).astype(BF16)
    b_r = jnp.pad(b_router, (0, LANES - N_EXPERTS), constant_values=NEG)[None, :]
    meta, gates, cnt = _router(h, w_r, b_r, tm)
    counts = cnt[0, :N_EXPERTS].astype(jnp.int32)
    padded = (counts + MOE_BLOCK - 1) // MOE_BLOCK * MOE_BLOCK
    ends_p = jnp.cumsum(padded)
    pstart = ends_p - padded
    n_blocks = -(-(r * TOP_K) // MOE_BLOCK) + N_EXPERTS
    n_slots = n_blocks * MOE_BLOCK
    block_start = jnp.arange(n_blocks, dtype=jnp.int32) * MOE_BLOCK
    block_e = jnp.minimum(jnp.sum((ends_p[None, :] <= block_start[:, None]).astype(jnp.int32), axis=1),
                          N_EXPERTS - 1)
    n_used = (ends_p[-1:] // MOE_BLOCK).astype(jnp.int32)
    dest = (pstart[meta[:, :TOP_K]] + meta[:, TOP_K:2 * TOP_K]).reshape(-1)
    xs = _dispatch(h, dest, jnp.zeros((n_slots, d), h.dtype) if xs_prev is None else xs_prev, tm)
    half = w_gu.shape[2] // 2
    b_gu_t = b_gu.reshape(N_EXPERTS, half // LANES, LANES, 2).transpose(0, 1, 3, 2)
    b_gu_t = b_gu_t.reshape(N_EXPERTS, 1, 2 * half)
    eo = _experts(xs, block_e, n_used, w_gu, b_gu_t, w_d, b_d[:, None, :])
    return _combine(eo, dest, gates, h, ln_g, ln_b, tm, lp), xs


def _pad_heads(w, heads, dim):
    k = w.shape[0]
    return jnp.pad(w.reshape(k, heads, dim), ((0, 0), (0, 0), (0, LANES - dim))).reshape(k, heads * LANES)


def _gate_weights(wg):
    k, n = wg.shape
    return (jnp.pad(wg, ((0, 0), (0, LANES - n))).astype(BF16),
            jnp.pad(wg.T, ((0, 16 - n), (0, 0))).astype(BF16))


def _even_layer(h, p, bsz, nc, tm, lp):
    r = h.shape[0]
    w_in = p['w_in']
    aq, av = A_HEADS * A_DQK, A_HEADS * A_DV
    o0 = 2 * aq + 2 * av
    w_main = jnp.concatenate([_pad_heads(w_in[:, :aq], A_HEADS, A_DQK),
                              _pad_heads(w_in[:, aq:2 * aq], A_HEADS, A_DQK),
                              w_in[:, 2 * aq:2 * aq + av], w_in[:, 2 * aq + av:o0],
                              w_in[:, o0 + 2 * A_HEADS:]], axis=1).astype(BF16)
    w_gate, w_gate_t = _gate_weights(w_in[:, o0:o0 + 2 * A_HEADS])
    y, gcol, grow = _in_proj(h, w_main, w_gate, w_gate_t, _chunk_tile(r // bsz, 704), w_main.shape[1])
    gb = p['gate_bias']
    bias_col = jnp.pad(gb, (0, LANES - gb.shape[0]))[None, :]
    bias_row = jnp.broadcast_to(jnp.pad(gb, (0, 16 - gb.shape[0]))[:, None], (16, CHUNK))
    ha = _mlstm(y, gcol, grow, bias_col, bias_row, p['head_norm'][None, :], bsz, nc)
    levels = max(1, math.ceil(math.log2(lp // S5_CHUNK)))
    ys = _s5(y, 4 * A_HEADS * LANES,
             _s5_params(p['a_re'], p['a_im'], p['log_step'], p['b_re'], p['b_im'],
                        p['c_re'], p['c_im'], p['d'], levels), bsz, lp)
    return _even_out(ha, ys, h, p['w_glu'].astype(BF16), p['b_glu'][None, :], p['w_out'].astype(BF16),
                     p['ln_g'][None, :], p['ln_b'][None, :], tm, lp)


def _odd_layer(h, p, bsz, nc, tm, lp):
    w_in = p['w_in']
    cw = C_HEADS * C_DK
    w_main = w_in[:, :4 * cw].astype(BF16)
    w_gate, w_gate_t = _gate_weights(w_in[:, 4 * cw:])
    y, gcol, grow = _in_proj(h, w_main, w_gate, w_gate_t, _chunk_tile(h.shape[0] // bsz, 704),
                             w_main.shape[1] // 2)
    neg_a = -jnp.exp(p['a_log'])
    zeros = jnp.zeros((C_HEADS,), F32)
    dt16 = jnp.concatenate([zeros, p['dt_bias']])
    na16 = jnp.concatenate([zeros, neg_a])
    pcol = jnp.pad(jnp.stack([dt16, na16]), ((0, 0), (0, LANES - 16)))
    prow = jnp.broadcast_to(jnp.stack([dt16, na16])[:, :, None], (2, 16, CHUNK))
    o = _gdn(y, gcol, grow, p['conv'], pcol, prow, p['norm'][None, :], bsz, nc)
    return _odd_out(o, h, p['w_out'].astype(BF16), p['ln_g'][None, :], p['ln_b'][None, :], tm, lp)


def kernel(x, meta_tokens, ln_g, ln_b, ev_w_in, ev_gate_bias, ev_head_norm, s5_a_re, s5_a_im, s5_log_step, s5_b_re, s5_b_im, s5_c_re, s5_c_im, s5_d, s5_w_glu, s5_b_glu, ev_w_out, od_w_in, od_conv, od_a_log, od_dt_bias, od_norm, od_w_out, moe_w_router, moe_b_router, moe_w_gate_up, moe_b_gate_up, moe_w_down, moe_b_down):
    bsz, seq, d = x.shape
    lp = PAD + N_META + seq
    assert lp % CHUNK == 0 and d == D_MODEL
    nc = lp // CHUNK
    r = bsz * lp
    assert r % MOE_BLOCK == 0 and lp >= MOE_BLOCK
    tm = 512 if (r % 512 == 0 and lp >= 512) else MOE_BLOCK
    meta = jnp.broadcast_to(meta_tokens[None], (bsz, N_META, d)).astype(x.dtype)
    h = jnp.concatenate([jnp.zeros((bsz, PAD, d), x.dtype), meta, x], axis=1).reshape(bsz * lp, d)
    xs = None
    for layer in range(ln_g.shape[0]):
        j = layer // 2
        if layer % 2 == 0:
            p = dict(w_in=ev_w_in[j], gate_bias=ev_gate_bias[j], head_norm=ev_head_norm[j],
                     a_re=s5_a_re[j], a_im=s5_a_im[j], log_step=s5_log_step[j], b_re=s5_b_re[j],
                     b_im=s5_b_im[j], c_re=s5_c_re[j], c_im=s5_c_im[j], d=s5_d[j],
                     w_glu=s5_w_glu[j], b_glu=s5_b_glu[j], w_out=ev_w_out[j],
                     ln_g=ln_g[layer, 0], ln_b=ln_b[layer, 0])
            h = _even_layer(h, p, bsz, nc, tm, lp)
        else:
            p = dict(w_in=od_w_in[j], conv=od_conv[j], a_log=od_a_log[j], dt_bias=od_dt_bias[j],
                     norm=od_norm[j], w_out=od_w_out[j], ln_g=ln_g[layer, 0], ln_b=ln_b[layer, 0])
            h = _odd_layer(h, p, bsz, nc, tm, lp)
        h, xs = _moe(h, moe_w_router[layer], moe_b_router[layer], moe_w_gate_up[layer],
                     moe_b_gate_up[layer], moe_w_down[layer], moe_b_down[layer],
                     ln_g[layer, 1][None, :], ln_b[layer, 1][None, :], MOE_BLOCK, lp, xs)
    return h.reshape(bsz, lp, d)[:, PAD + N_META:]
```

```python
import functools
import math

import jax
import jax.numpy as jnp
from jax import lax
from jax.experimental import pallas as pl
from jax.experimental.pallas import tpu as pltpu

F32 = jnp.float32
BF16 = jnp.bfloat16

D_MODEL = 1024
DEPTH = 4
N_META = 16
CHUNK = 64
PAD = CHUNK - N_META
NEG = -1e30
LN_EPS = 1e-5
DN_ALPHA = (2.0 * DEPTH) ** 0.25

A_HEADS = 4
A_DQK = D_MODEL // 16
A_DV = D_MODEL // 8
A_GATE_CAP = 15.0
B_CH = D_MODEL // 2
B_GROUP = 16
B_GROUPS = B_CH // B_GROUP
B_STATE = 64
S5_CHUNK = 16
C_HEADS = D_MODEL // 128
C_DK = 128
C_CONV = 4
N_EXPERTS = 32
TOP_K = 4
SWIGLU_LIMIT = 7.0
SWIGLU_ALPHA = 1.702
MOE_BLOCK = 256

LANES = 128
VMEM_LIMIT = 56 * 1024 * 1024


def _cparams(*sem):
    return pltpu.CompilerParams(dimension_semantics=sem, vmem_limit_bytes=VMEM_LIMIT)


def _chunk_tile(lp, target):
    best = CHUNK
    for t in range(CHUNK, target + 1, CHUNK):
        if lp % t == 0:
            best = t
    return best


def _dot(a, b):
    return jnp.dot(a, b, preferred_element_type=F32)


def _dot_nt(a, b):
    return lax.dot_general(a, b, (((1,), (1,)), ((), ())), preferred_element_type=F32)


def _dot_tn(a, b):
    return lax.dot_general(a, b, (((0,), (0,)), ((), ())), preferred_element_type=F32)


def _split3(x):
    hi = x.astype(BF16)
    r1 = x - hi.astype(F32)
    mid = r1.astype(BF16)
    lo = (r1 - mid.astype(F32)).astype(BF16)
    return hi, mid, lo


def _dot01_left(t01, x):
    hi, mid, lo = _split3(x)
    return _dot(t01, hi) + _dot(t01, mid) + _dot(t01, lo)


def _dot01_right(x, t01):
    hi, mid, lo = _split3(x)
    return _dot(hi, t01) + _dot(mid, t01) + _dot(lo, t01)


def _row_sum_lanes(x):
    ones = jnp.ones((x.shape[1], LANES), BF16)
    hi = x.astype(BF16)
    lo = (x - hi.astype(F32)).astype(BF16)
    return _dot(hi, ones) + _dot(lo, ones)


def _sigmoid(x):
    return 1.0 / (1.0 + jnp.exp(-x))


def _softplus(x):
    return jnp.maximum(x, 0.0) + jnp.log(1.0 + jnp.exp(-jnp.abs(x)))


def _log_sigmoid(x):
    return -_softplus(-x)


def _silu(x):
    return x * _sigmoid(x)


def _gelu_tanh(x):
    c = math.sqrt(2.0 / math.pi)
    return 0.5 * x * (1.0 + jnp.tanh(c * (x + 0.044715 * (x * x * x))))


def _iota(shape, dim):
    return lax.broadcasted_iota(jnp.int32, shape, dim)


def _tri_masks(n):
    r = _iota((n, n), 0)
    c = _iota((n, n), 1)
    return r, c


def _proj_kernel(x_ref, w_ref, wg_ref, wgt_ref, y_ref, gcol_ref, grow_ref):
    xb = x_ref[...].astype(BF16)
    y_ref[...] = _dot(xb, w_ref[...])

    @pl.when(pl.program_id(1) == 0)
    def _():
        gcol_ref[...] = _dot(xb, wg_ref[...])
        grow = _dot_nt(wgt_ref[...], xb)
        for j in range(grow_ref.shape[0]):
            grow_ref[j] = grow[:, j * CHUNK:(j + 1) * CHUNK]


def _in_proj(h, w_main, w_gate, w_gate_t, tm, tn):
    r, d = h.shape
    n = w_main.shape[1]
    cpt = tm // CHUNK
    return pl.pallas_call(
        _proj_kernel,
        grid=(r // tm, n // tn),
        in_specs=[pl.BlockSpec((tm, d), lambda i, j: (i, 0)),
                  pl.BlockSpec((d, tn), lambda i, j: (0, j)),
                  pl.BlockSpec((d, LANES), lambda i, j: (0, 0)),
                  pl.BlockSpec((16, d), lambda i, j: (0, 0))],
        out_specs=[pl.BlockSpec((tm, tn), lambda i, j: (i, j)),
                   pl.BlockSpec((tm, LANES), lambda i, j: (i, 0)),
                   pl.BlockSpec((cpt, 16, CHUNK), lambda i, j: (i, 0, 0))],
        out_shape=[jax.ShapeDtypeStruct((r, n), F32),
                   jax.ShapeDtypeStruct((r, LANES), F32),
                   jax.ShapeDtypeStruct((r // CHUNK, 16, CHUNK), F32)],
        compiler_params=_cparams("parallel", "arbitrary"),
        name="in_proj",
    )(h, w_main, w_gate, w_gate_t)


def _mlstm_kernel(q_ref, k_ref, v_ref, o_ref, gcol_ref, grow_ref, bcol_ref, brow_ref, hn_ref,
                  out_ref, c_scr, m_scr):
    c = pl.program_id(1)

    @pl.when(c == 0)
    def _():
        c_scr[...] = jnp.zeros_like(c_scr)
        m_scr[...] = jnp.zeros_like(m_scr)

    n = CHUNK
    ri, ci = _tri_masks(n)
    causal = ri >= ci
    tri_l = jnp.where(causal, 1.0, 0.0).astype(BF16)
    tri_u = jnp.where(ri <= ci, 1.0, 0.0).astype(BF16)
    first = c == 0
    pad_c = jnp.logical_and(first, _iota((n, 1), 0) < PAD)
    pad_r = jnp.logical_and(first, _iota((1, n), 1) < PAD)

    i_c, b_c, i_r, b_r = [], [], [], []
    for sq in range(q_ref.shape[0]):
        gc = A_GATE_CAP * jnp.tanh((gcol_ref[sq] + bcol_ref[...]) * (1.0 / A_GATE_CAP))
        i_c.append(jnp.where(pad_c, NEG, gc))
        b_c.append(_dot01_left(tri_l, jnp.where(pad_c, 0.0, _log_sigmoid(gc))))
        gr = A_GATE_CAP * jnp.tanh((grow_ref[sq] + brow_ref[...]) * (1.0 / A_GATE_CAP))
        i_r.append(jnp.where(pad_r, NEG, gr))
        b_r.append(_dot01_right(jnp.where(pad_r, 0.0, _log_sigmoid(gr)), tri_u))

    one_col = jnp.where(_iota((n, A_DV), 1) == 0, 1.0, 0.0).astype(BF16)
    chains = [(sq, hd) for sq in range(q_ref.shape[0]) for hd in range(A_HEADS)]
    heads = range(len(chains))
    sl = [slice(hd * LANES, (hd + 1) * LANES) for _, hd in chains]
    q = [q_ref[sq, :, sl[h]].astype(BF16) for h, (sq, _) in enumerate(chains)]
    kf = [k_ref[sq, :, sl[h]] * (A_DQK ** -0.5) for h, (sq, _) in enumerate(chains)]
    vext = [jnp.concatenate([v_ref[sq, :, sl[h]].astype(BF16), one_col], axis=1)
            for h, (sq, _) in enumerate(chains)]
    bi = [b_c[sq][:, A_HEADS + hd:A_HEADS + hd + 1] for sq, hd in chains]
    ii = [i_c[sq][:, hd:hd + 1] for sq, hd in chains]
    bj = [b_r[sq][A_HEADS + hd:A_HEADS + hd + 1, :] for sq, hd in chains]
    ij = [i_r[sq][hd:hd + 1, :] for sq, hd in chains]
    b_last = [t[n - 1:n, :] for t in bi]
    a_end = [b_last[h] - bi[h] + ii[h] for h in heads]
    m_loc = [jnp.max(t, axis=0, keepdims=True) for t in a_end]
    m0 = [m_scr[h][0:1, 0:1] for h in heads]
    d_intra = [jnp.where(causal, bi[h] - bj[h] + ij[h], NEG) for h in heads]
    d_inter = [bi[h] + m0[h] for h in heads]
    m_row = [jnp.maximum(jnp.max(d_intra[h], axis=1, keepdims=True), d_inter[h]) for h in heads]
    s = [jnp.exp(d_inter[h] - m_row[h]) for h in heads]
    qk = [(_dot_nt(q[h], kf[h].astype(BF16)) * jnp.exp(d_intra[h] - m_row[h])).astype(BF16) for h in heads]
    cext = [c_scr[h] for h in heads]
    num_ext = [_dot(qk[h], vext[h]) + s[h] * _dot(q[h], cext[h].astype(BF16)) for h in heads]
    w_end = [jnp.exp(a_end[h] - m_loc[h]) for h in heads]
    d_ext = [_dot_tn((kf[h] * w_end[h]).astype(BF16), vext[h]) for h in heads]
    for h in heads:
        m_new = jnp.maximum(b_last[h] + m0[h], m_loc[h])
        c_scr[h] = jnp.exp(b_last[h] + m0[h] - m_new) * cext[h] + jnp.exp(m_loc[h] - m_new) * d_ext[h]
        m_scr[h] = jnp.broadcast_to(m_new, (8, LANES))
    hh = [num_ext[h][:, :A_DV] / jnp.maximum(jnp.abs(num_ext[h][:, A_DV:A_DV + 1]), jnp.exp(-m_row[h]))
          for h in heads]
    mu = [_row_sum_lanes(t) * (1.0 / A_DV) for t in hh]
    hc = [hh[h] - mu[h] for h in heads]
    var = [_row_sum_lanes(t * t) * (1.0 / A_DV) for t in hc]
    for h, (sq, _) in enumerate(chains):
        hnorm = hc[h] * lax.rsqrt(var[h] + 1e-6) * hn_ref[:, sl[h]]
        out_ref[sq, :, sl[h]] = _sigmoid(o_ref[sq, :, sl[h]]) * hnorm


MLSTM_SEQS_PER_STEP = 2


def _mlstm(y, gcol, grow3, bias_col, bias_row, head_norm, bsz, nc):
    r = y.shape[0]
    lp = r // bsz
    w = A_HEADS * LANES
    ns = MLSTM_SEQS_PER_STEP if bsz % MLSTM_SEQS_PER_STEP == 0 else 1
    y4 = y.reshape(bsz // ns, ns, lp, y.shape[1])
    col = lambda j: pl.BlockSpec((None, ns, CHUNK, w), lambda b, c: (b, 0, c, j))
    out = pl.pallas_call(
        _mlstm_kernel,
        grid=(bsz // ns, nc),
        in_specs=[col(0), col(1), col(2), col(3),
                  pl.BlockSpec((None, ns, CHUNK, LANES), lambda b, c: (b, 0, c, 0)),
                  pl.BlockSpec((None, ns, None, 16, CHUNK), lambda b, c: (b, 0, c, 0, 0)),
                  pl.BlockSpec((1, LANES), lambda b, c: (0, 0)),
                  pl.BlockSpec((16, CHUNK), lambda b, c: (0, 0)),
                  pl.BlockSpec((1, w), lambda b, c: (0, 0))],
        out_specs=pl.BlockSpec((None, ns, CHUNK, w), lambda b, c: (b, 0, c, 0)),
        out_shape=jax.ShapeDtypeStruct((bsz // ns, ns, lp, w), F32),
        scratch_shapes=[pltpu.VMEM((ns * A_HEADS, LANES, 2 * LANES), F32),
                        pltpu.VMEM((ns * A_HEADS, 8, LANES), F32)],
        compiler_params=_cparams("parallel", "arbitrary"),
        name="mlstm",
    )(y4, y4, y4, y4, gcol.reshape(bsz // ns, ns, lp, LANES), grow3.reshape(bsz // ns, ns, nc, 16, CHUNK),
      bias_col, bias_row, head_norm)
    return out.reshape(r, w)


def _s5_params(a_re, a_im, log_step, b_re, b_im, c_re, c_im, d_skip, levels):
    hp = lax.Precision.HIGHEST
    g, p = a_re.shape
    dt = jnp.exp(log_step)[:, None]
    mag = jnp.exp(a_re * dt)
    lb_re, lb_im = mag * jnp.cos(a_im * dt), mag * jnp.sin(a_im * dt)
    inv = 1.0 / (a_re * a_re + a_im * a_im)
    zr, zi = lb_re - 1.0, lb_im
    fr = (zr * a_re + zi * a_im) * inv
    fi = (zi * a_re - zr * a_im) * inv
    bb_re = fr[..., None] * b_re - fi[..., None] * b_im
    bb_im = fr[..., None] * b_im + fi[..., None] * b_re
    n = S5_CHUNK
    tau = jnp.arange(n + 1, dtype=F32)[:, None, None]
    pm = jnp.exp(tau * (a_re * dt))
    pr, pi = pm * jnp.cos(tau * (a_im * dt)), pm * jnp.sin(tau * (a_im * dt))
    e_re = pr[..., None] * bb_re - pi[..., None] * bb_im
    e_im = pr[..., None] * bb_im + pi[..., None] * bb_re
    kern = (jnp.einsum('gcp,tgpd->tgcd', c_re, e_re[:n], precision=hp)
            - jnp.einsum('gcp,tgpd->tgcd', c_im, e_im[:n], precision=hp))
    idx = jnp.arange(n)
    diff = idx[None, :] - idx[:, None]
    kd = jnp.where((diff >= 0)[:, :, None, None, None], kern[jnp.clip(diff, 0, n - 1)], 0.0)
    m_t = kd.transpose(2, 0, 4, 1, 3).reshape(g, n * B_GROUP, n * B_GROUP)
    w_re = e_re[n - 1 - idx].transpose(1, 0, 3, 2)
    w_im = e_im[n - 1 - idx].transpose(1, 0, 3, 2)
    w_t = jnp.concatenate([w_re, w_im], axis=-1).reshape(g, n * B_GROUP, 2 * p)
    f_re = c_re[None] * pr[1:, :, None, :] - c_im[None] * pi[1:, :, None, :]
    f_im = c_re[None] * pi[1:, :, None, :] + c_im[None] * pr[1:, :, None, :]
    v_t = jnp.concatenate([f_re.transpose(1, 3, 0, 2), -f_im.transpose(1, 3, 0, 2)], axis=1)
    v_t = v_t.reshape(g, 2 * p, n * B_GROUP)
    steps = (n * 2.0 ** jnp.arange(levels, dtype=F32))[:, None, None]
    sm = jnp.exp(steps * (a_re * dt))
    sr, si = sm * jnp.cos(steps * (a_im * dt)), sm * jnp.sin(steps * (a_im * dt))
    lam_a = jnp.concatenate([sr, sr], axis=-1)[:, :, None, :]
    lam_b = jnp.concatenate([-si, si], axis=-1)[:, :, None, :]
    d_flat = jnp.tile(d_skip, (1, n))[:, None, :]
    return m_t.astype(BF16), w_t.astype(BF16), v_t.astype(BF16), lam_a, lam_b, d_flat


S5_GROUPS_PER_STEP = LANES // B_GROUP
S5_SCAN_OFF = 128


def _s5_kernel(u_ref, mt_ref, wt_ref, vt_ref, d_ref, la_ref, lb_ref, out_ref, scan_scr, *, nch):
    n = S5_CHUNK
    gps = S5_GROUPS_PER_STEP
    rows = 2 * nch
    levels = la_ref.shape[0]
    lane_blk = _iota((rows, LANES), 1) // B_GROUP
    in_blk = [lane_blk == i for i in range(gps)]
    r = _iota((rows, 1), 0)
    r_in = jnp.where(r >= nch, r - nch, r)
    xs = [u_ref[pl.ds(s, rows, stride=n), :] for s in range(n)]

    def shifted(x, lanes):
        return pltpu.roll(x, lanes % LANES, 1) if lanes % LANES else x

    ys = []
    for g in range(gps):
        halves = []
        for half in range(2):
            acc = jnp.zeros((rows, LANES), F32)
            for sp in range(gps):
                acc = jnp.where(in_blk[sp], shifted(xs[half * gps + sp], B_GROUP * (sp - g)), acc)
            halves.append(acc)
        u = jnp.concatenate(halves, axis=1)
        ub = u.astype(BF16)
        y = _dot(ub, mt_ref[g]) + d_ref[g] * u
        x = _dot(ub, wt_ref[g])
        scan_scr[g, 0:S5_SCAN_OFF, :] = jnp.zeros((S5_SCAN_OFF, LANES), F32)
        for lv in range(levels):
            sh = 1 << lv
            scan_scr[g, S5_SCAN_OFF:S5_SCAN_OFF + rows, :] = x
            prev = scan_scr[g, S5_SCAN_OFF - sh:S5_SCAN_OFF - sh + rows, :]
            prev = jnp.where(r_in >= sh, prev, 0.0)
            x = x + la_ref[lv, g] * prev + lb_ref[lv, g] * pltpu.roll(prev, B_STATE, 1)
        scan_scr[g, S5_SCAN_OFF:S5_SCAN_OFF + rows, :] = x
        x0 = scan_scr[g, S5_SCAN_OFF - 1:S5_SCAN_OFF - 1 + rows, :]
        x0 = jnp.where(r_in >= 1, x0, 0.0)
        ys.append(y + _dot(x0.astype(BF16), vt_ref[g]))
    for s in range(n):
        half, sp = divmod(s, gps)
        acc = jnp.zeros((rows, LANES), F32)
        for g in range(gps):
            acc = jnp.where(in_blk[g], shifted(ys[g][:, half * LANES:(half + 1) * LANES], B_GROUP * (g - sp)), acc)
        out_ref[pl.ds(s, rows, stride=n), :] = acc


def _s5(y, u_col0, params, bsz, lp):
    m_t, w_t, v_t, lam_a, lam_b, d_flat = params
    r = y.shape[0]
    nch = lp // S5_CHUNK
    gps = S5_GROUPS_PER_STEP
    wide = S5_CHUNK * B_GROUP
    st = 2 * B_STATE
    levels = lam_a.shape[0]
    assert bsz % 2 == 0 and (1 << (levels - 1)) <= S5_SCAN_OFF and u_col0 % LANES == 0
    gspec = lambda *shape: pl.BlockSpec((gps,) + shape, lambda j, b: (j, 0, 0))
    lspec = pl.BlockSpec((levels, gps, 1, st), lambda j, b: (0, j, 0, 0))
    return pl.pallas_call(
        functools.partial(_s5_kernel, nch=nch),
        grid=(B_GROUPS // gps, bsz // 2),
        in_specs=[pl.BlockSpec((2 * lp, LANES), lambda j, b: (b, u_col0 // LANES + j)),
                  gspec(wide, wide), gspec(wide, st), gspec(st, wide), gspec(1, wide), lspec, lspec],
        out_specs=pl.BlockSpec((2 * lp, LANES), lambda j, b: (b, j)),
        out_shape=jax.ShapeDtypeStruct((r, B_CH), F32),
        scratch_shapes=[pltpu.VMEM((gps, S5_SCAN_OFF + 2 * nch, LANES), F32)],
        compiler_params=_cparams("parallel", "arbitrary"),
        name="s5",
    )(y, m_t, w_t, v_t, d_flat, lam_a, lam_b)


def _res_ln(h, mix, g, b, pad_rows):
    z = DN_ALPHA * h + mix
    mu = jnp.mean(z, axis=1, keepdims=True)
    zc = z - mu
    var = jnp.mean(zc * zc, axis=1, keepdims=True)
    out = zc * lax.rsqrt(var + LN_EPS) * g + b
    return jnp.where(pad_rows, 0.0, out)


def _pad_rows_mask(tm, lp):
    start = (pl.program_id(0) * tm) % lp
    pos = start + _iota((tm, 1), 0)
    pos = jnp.where(pos >= lp, pos - lp, pos)
    return pos < PAD


def _even_out_kernel(ha_ref, ys_ref, h_ref, wglu_ref, bglu_ref, wout_ref, g_ref, b_ref, out_ref,
                     *, lp):
    tm = h_ref.shape[0]
    yb = _gelu_tanh(ys_ref[...])
    hb = yb * _sigmoid(_dot(yb.astype(BF16), wglu_ref[...]) + bglu_ref[...])
    av = ha_ref.shape[1]
    mix = _dot(ha_ref[...].astype(BF16), wout_ref[:av, :]) + _dot(hb.astype(BF16), wout_ref[av:, :])
    out_ref[...] = _res_ln(h_ref[...], mix, g_ref[...], b_ref[...], _pad_rows_mask(tm, lp))


def _even_out(ha, ys, h, w_glu, b_glu, w_out, ln_g, ln_b, tm, lp):
    r, d = h.shape
    av, bc = ha.shape[1], ys.shape[1]
    full = lambda shape: pl.BlockSpec(shape, lambda i: (0, 0))
    return pl.pallas_call(
        functools.partial(_even_out_kernel, lp=lp),
        grid=(r // tm,),
        in_specs=[pl.BlockSpec((tm, av), lambda i: (i, 0)),
                  pl.BlockSpec((tm, bc), lambda i: (i, 0)),
                  pl.BlockSpec((tm, d), lambda i: (i, 0)),
                  full((bc, bc)), full((1, bc)), full((av + bc, d)), full((1, d)), full((1, d))],
        out_specs=pl.BlockSpec((tm, d), lambda i: (i, 0)),
        out_shape=jax.ShapeDtypeStruct((r, d), F32),
        compiler_params=_cparams("parallel"),
        name="even_out",
    )(ha, ys, h, w_glu, b_glu, w_out, ln_g, ln_b)


def _gdn_kernel(q_ref, k_ref, v_ref, z_ref, gcol_ref, grow_ref, conv_ref, pcol_ref, prow_ref,
                nw_ref, out_ref, s_scr, carry_scr, ext_scr):
    c = pl.program_id(1)

    @pl.when(c == 0)
    def _():
        s_scr[...] = jnp.zeros_like(s_scr)
        carry_scr[...] = jnp.zeros_like(carry_scr)

    n = CHUNK
    ri, ci = _tri_masks(n)
    incl = ri >= ci
    strict = ri > ci
    tri_l = jnp.where(incl, 1.0, 0.0).astype(BF16)
    tri_u = jnp.where(ri <= ci, 1.0, 0.0).astype(BF16)
    first = c == 0
    pad_c = jnp.logical_and(first, _iota((n, 1), 0) < PAD)
    pad_r = jnp.logical_and(first, _iota((1, n), 1) < PAD)

    gc = gcol_ref[...]
    beta_c = jnp.where(pad_c, 0.0, _sigmoid(gc))
    g_c = jnp.where(pad_c, 0.0, pcol_ref[1:2, :] * _softplus(gc + pcol_ref[0:1, :]))
    gcum_c = _dot01_left(tri_l, g_c)
    gr = grow_ref[0]
    g_r = jnp.where(pad_r, 0.0, prow_ref[1] * _softplus(gr + prow_ref[0]))
    gcum_r = _dot01_right(g_r, tri_u)

    width = q_ref.shape[1]
    ext_scr[0:8, :] = carry_scr[...]
    for j, ref in enumerate((q_ref, k_ref, v_ref)):
        ext_scr[8:8 + n, j * width:(j + 1) * width] = ref[...]
    carry_scr[...] = ext_scr[n:n + 8, :]

    def conv(col):
        acc = conv_ref[0:1, col] * ext_scr[5:5 + n, col]
        for j in range(1, C_CONV):
            acc = acc + conv_ref[j:j + 1, col] * ext_scr[5 + j:5 + j + n, col]
        return _silu(acc)

    def l2n(x):
        return x * lax.rsqrt(jnp.sum(x * x, axis=1, keepdims=True) + 1e-6)

    heads = range(C_HEADS)
    lanes = lambda base, h: slice(base + h * LANES, base + (h + 1) * LANES)
    q = [l2n(conv(lanes(0, h))) * (C_DK ** -0.5) for h in heads]
    k = [l2n(conv(lanes(width, h))) for h in heads]
    v = [conv(lanes(2 * width, h)) for h in heads]
    beta = [beta_c[:, h:h + 1] for h in heads]
    gi = [gcum_c[:, C_HEADS + h:C_HEADS + h + 1] for h in heads]
    gj = [gcum_r[C_HEADS + h:C_HEADS + h + 1, :] for h in heads]
    g_last = [g[n - 1:n, :] for g in gi]
    decay = [jnp.where(incl, jnp.exp(jnp.where(incl, gi[h] - gj[h], 0.0)), 0.0) for h in heads]
    eg = [jnp.exp(g) for g in gi]
    kb = [t.astype(BF16) for t in k]
    qkk = [_dot_nt(jnp.concatenate([q[h].astype(BF16), kb[h]], axis=0), kb[h]) for h in heads]
    attn = [(qkk[h][:n] * decay[h]).astype(BF16) for h in heads]
    ab = [jnp.where(strict, -(beta[h] * qkk[h][n:] * decay[h]), 0.0).astype(BF16) for h in heads]
    x = [jnp.concatenate([beta[h] * v[h], (beta[h] * eg[h]) * k[h]], axis=1) for h in heads]
    x = [x[h] + _dot(ab[h], x[h].astype(BF16)) for h in heads]
    for _ in range(5):
        ab = [_dot(t, t).astype(BF16) for t in ab]
        x = [x[h] + _dot(ab[h], x[h].astype(BF16)) for h in heads]
    s0 = [s_scr[h] for h in heads]
    q_dec = [(q[h] * eg[h]).astype(BF16) for h in heads]
    k_dec = [(k[h] * jnp.exp(g_last[h] - gi[h])).astype(BF16) for h in heads]
    ws_qs = [_dot(jnp.concatenate([x[h][:, LANES:].astype(BF16), q_dec[h]], axis=0), s0[h].astype(BF16))
             for h in heads]
    vb = [(x[h][:, :LANES] - ws_qs[h][:n]).astype(BF16) for h in heads]
    o = [ws_qs[h][n:] + _dot(attn[h], vb[h]) for h in heads]
    for h in heads:
        s_scr[h] = jnp.exp(g_last[h]) * s0[h] + _dot_tn(k_dec[h], vb[h])
    ms = [jnp.mean(t * t, axis=1, keepdims=True) for t in o]
    for h in heads:
        on = o[h] * lax.rsqrt(ms[h] + 1e-6) * nw_ref[...]
        out_ref[:, lanes(0, h)] = on * _silu(z_ref[:, lanes(0, h)])


def _gdn(y, gcol, grow3, conv_w, pcol, prow, norm_w, bsz, nc):
    r = y.shape[0]
    w = C_HEADS * C_DK
    row = lambda b, c: b * nc + c
    return pl.pallas_call(
        _gdn_kernel,
        grid=(bsz, nc),
        in_specs=[pl.BlockSpec((CHUNK, w), lambda b, c: (row(b, c), 0)),
                  pl.BlockSpec((CHUNK, w), lambda b, c: (row(b, c), 1)),
                  pl.BlockSpec((CHUNK, w), lambda b, c: (row(b, c), 2)),
                  pl.BlockSpec((CHUNK, w), lambda b, c: (row(b, c), 3)),
                  pl.BlockSpec((CHUNK, LANES), lambda b, c: (row(b, c), 0)),
                  pl.BlockSpec((1, 16, CHUNK), lambda b, c: (row(b, c), 0, 0)),
                  pl.BlockSpec((C_CONV, 3 * w), lambda b, c: (0, 0)),
                  pl.BlockSpec((2, LANES), lambda b, c: (0, 0)),
                  pl.BlockSpec((2, 16, CHUNK), lambda b, c: (0, 0, 0)),
                  pl.BlockSpec((1, LANES), lambda b, c: (0, 0))],
        out_specs=pl.BlockSpec((CHUNK, w), lambda b, c: (row(b, c), 0)),
        out_shape=jax.ShapeDtypeStruct((r, w), F32),
        scratch_shapes=[pltpu.VMEM((C_HEADS, C_DK, C_DK), F32),
                        pltpu.VMEM((8, 3 * w), F32),
                        pltpu.VMEM((CHUNK + 8, 3 * w), F32)],
        compiler_params=_cparams("parallel", "arbitrary"),
        name="gdn",
    )(y, y, y, y, gcol, grow3, conv_w, pcol, prow, norm_w)


def _odd_out_kernel(o_ref, h_ref, wout_ref, g_ref, b_ref, out_ref, *, lp):
    tm = h_ref.shape[0]
    mix = _dot(o_ref[...].astype(BF16), wout_ref[...])
    out_ref[...] = _res_ln(h_ref[...], mix, g_ref[...], b_ref[...], _pad_rows_mask(tm, lp))


def _odd_out(o, h, w_out, ln_g, ln_b, tm, lp):
    r, d = h.shape
    full = lambda shape: pl.BlockSpec(shape, lambda i: (0, 0))
    return pl.pallas_call(
        functools.partial(_odd_out_kernel, lp=lp),
        grid=(r // tm,),
        in_specs=[pl.BlockSpec((tm, d), lambda i: (i, 0)),
                  pl.BlockSpec((tm, d), lambda i: (i, 0)),
                  full((d, d)), full((1, d)), full((1, d))],
        out_specs=pl.BlockSpec((tm, d), lambda i: (i, 0)),
        out_shape=jax.ShapeDtypeStruct((r, d), F32),
        compiler_params=_cparams("parallel"),
        name="odd_out",
    )(o, h, w_out, ln_g, ln_b)


def _router_kernel(x_ref, w_ref, b_ref, meta_ref, gate_ref, cnt_ref, base_scr):
    @pl.when(pl.program_id(0) == 0)
    def _():
        base_scr[...] = jnp.zeros_like(base_scr)

    tm = x_ref.shape[0]
    logits = _dot(x_ref[...].astype(BF16), w_ref[...]) + b_ref[...]
    lane = _iota((tm, LANES), 1)
    lane_f = lane.astype(F32)
    work = logits
    vals, sels = [], []
    onehot = jnp.zeros((tm, LANES), F32)
    for _ in range(TOP_K):
        m = jnp.max(work, axis=1, keepdims=True)
        idx = jnp.min(jnp.where(work == m, lane_f, float(LANES)), axis=1, keepdims=True)
        sel = lane_f == idx
        vals.append(m)
        sels.append((sel, idx))
        onehot = onehot + jnp.where(sel, 1.0, 0.0)
        work = jnp.where(sel, -jnp.inf, work)
    ri, ci = _tri_masks(tm)
    tri = jnp.where(ri > ci, 1.0, 0.0).astype(BF16)
    before = _dot(tri, onehot.astype(BF16)) + base_scr[0:1, :]
    base_scr[...] = base_scr[...] + jnp.sum(onehot, axis=0, keepdims=True)
    cnt_ref[...] = base_scr[...]
    es = [jnp.exp(v - vals[0]) for v in vals]
    tot = es[0] + es[1] + es[2] + es[3]
    meta = jnp.zeros((tm, LANES), jnp.int32)
    gate = jnp.zeros((tm, LANES), F32)
    for k, (sel, idx) in enumerate(sels):
        rank = jnp.sum(jnp.where(sel, before, 0.0), axis=1, keepdims=True)
        meta = jnp.where(lane == k, idx.astype(jnp.int32), meta)
        meta = jnp.where(lane == TOP_K + k, rank.astype(jnp.int32), meta)
        gate = jnp.where(lane == k, es[k] / tot, gate)
    meta_ref[...] = meta
    gate_ref[...] = gate


def _router(h, w_r, b_r, tm):
    r, d = h.shape
    return pl.pallas_call(
        _router_kernel,
        grid=(r // tm,),
        in_specs=[pl.BlockSpec((tm, d), lambda i: (i, 0)),
                  pl.BlockSpec((d, LANES), lambda i: (0, 0)),
                  pl.BlockSpec((1, LANES), lambda i: (0, 0))],
        out_specs=[pl.BlockSpec((tm, LANES), lambda i: (i, 0)),
                   pl.BlockSpec((tm, LANES), lambda i: (i, 0)),
                   pl.BlockSpec((8, LANES), lambda i: (0, 0))],
        out_shape=[jax.ShapeDtypeStruct((r, LANES), jnp.int32),
                   jax.ShapeDtypeStruct((r, LANES), F32),
                   jax.ShapeDtypeStruct((8, LANES), F32)],
        scratch_shapes=[pltpu.VMEM((8, LANES), F32)],
        compiler_params=_cparams("arbitrary"),
        name="router",
    )(h, w_r, b_r)


ROW_DMA_UNROLL = 8
ROW_DMA_WAIT_GROUP = 64


def _issue_row_copies(make_copy, tm):
    def issue(i, carry):
        for u in range(ROW_DMA_UNROLL):
            for k in range(TOP_K):
                make_copy(i * ROW_DMA_UNROLL + u, k).start(priority=(u * TOP_K + k) % 2)
        return carry

    lax.fori_loop(0, tm // ROW_DMA_UNROLL, issue, 0)


def _wait_row_copies(make_copy, tm):
    def drain(i, carry):
        for _ in range(ROW_DMA_WAIT_GROUP):
            make_copy(0, 0).wait()
        return carry

    lax.fori_loop(0, tm * TOP_K // ROW_DMA_WAIT_GROUP, drain, 0)


def _dispatch_kernel(dest_ref, x_hbm, xs_in, xs_hbm, stage, sem_in, sem_out, *, tm):
    del xs_in
    i = pl.program_id(0)
    n = pl.num_programs(0)
    cur = i % 3

    def load(tile, into):
        return pltpu.make_async_copy(x_hbm.at[pl.ds(tile * tm, tm)], stage.at[into], sem_in.at[into])

    def scatter(src_slot, sem_slot):
        def make_copy(t, k):
            return pltpu.make_async_copy(stage.at[src_slot, pl.ds(t, 1)],
                                         xs_hbm.at[pl.ds(dest_ref[t * TOP_K + k], 1)], sem_out.at[sem_slot])
        return make_copy

    @pl.when(i == 0)
    def _():
        load(0, 0).start()

    @pl.when(i + 1 < n)
    def _():
        load(i + 1, (i + 1) % 3).start()

    load(i, cur).wait()
    _issue_row_copies(scatter(cur, i % 2), tm)

    @pl.when(i >= 1)
    def _():
        _wait_row_copies(scatter((i + 2) % 3, (i + 1) % 2), tm)

    @pl.when(i == n - 1)
    def _():
        _wait_row_copies(scatter(cur, i % 2), tm)


def _dispatch(h, dest_flat, xs0, tm):
    r, d = h.shape
    n_slots = xs0.shape[0]
    return pl.pallas_call(
        functools.partial(_dispatch_kernel, tm=tm),
        grid=(r // tm,),
        in_specs=[pl.BlockSpec((tm * TOP_K,), lambda i: (i,), memory_space=pltpu.SMEM),
                  pl.BlockSpec(memory_space=pl.ANY),
                  pl.BlockSpec(memory_space=pl.ANY)],
        out_specs=pl.BlockSpec(memory_space=pl.ANY),
        out_shape=jax.ShapeDtypeStruct((n_slots, d), h.dtype),
        scratch_shapes=[pltpu.VMEM((3, tm, d), h.dtype), pltpu.SemaphoreType.DMA((3,)),
                        pltpu.SemaphoreType.DMA((2,))],
        input_output_aliases={2: 0},
        compiler_params=_cparams("arbitrary"),
        name="moe_dispatch",
    )(dest_flat, h, xs0)


def _expert_kernel(be_ref, nu_ref, slot_ref, nxt_ref, x_ref, bgu_ref, bd_ref, wgu_hbm, wd_hbm, out_ref,
                   wgu_buf, wd_buf, wgu_scr, wd_scr, sems, *, layer):
    b = pl.program_id(0)
    used = b < nu_ref[0]
    e = be_ref[b]
    fresh = jnp.logical_and(used, jnp.logical_or(b == 0, e != be_ref[jnp.maximum(b - 1, 0)]))
    slot = slot_ref[b]
    tile = 2 * LANES
    n_tiles = wgu_buf.shape[2] // tile

    def weight_copies(expert, into):
        return (pltpu.make_async_copy(wgu_hbm.at[layer, expert], wgu_buf.at[into], sems.at[0, into]),
                pltpu.make_async_copy(wd_hbm.at[layer, expert], wd_buf.at[into], sems.at[1, into]))

    @pl.when(jnp.logical_and(used, b == 0))
    def _():
        for cp in weight_copies(e, slot):
            cp.start()

    @pl.when(fresh)
    def _():
        for cp in weight_copies(e, slot):
            cp.wait()

        @pl.when(nxt_ref[b] >= 0)
        def _():
            for cp in weight_copies(nxt_ref[b], 1 - slot):
                cp.start()

        r = _iota((tile, tile), 0)
        c = _iota((tile, tile), 1)
        src = jnp.where(c < LANES, 2 * c, 2 * (c - LANES) + 1)
        perm = jnp.where(r == src, 1.0, 0.0).astype(BF16)
        for j in range(n_tiles):
            cols = slice(j * tile, (j + 1) * tile)
            wgu_scr[:, cols] = _dot(wgu_buf[slot, :, cols].astype(BF16), perm).astype(BF16)
        wd_scr[...] = wd_buf[slot].astype(BF16)

    @pl.when(used)
    def _():
        xb = x_ref[...].astype(BF16)
        h = _dot(xb, wgu_scr[...]) + bgu_ref[0]
        acts = []
        for j in range(n_tiles):
            gate = jnp.minimum(h[:, j * tile:j * tile + LANES], SWIGLU_LIMIT)
            up = jnp.clip(h[:, j * tile + LANES:(j + 1) * tile], -SWIGLU_LIMIT, SWIGLU_LIMIT)
            acts.append(((up + 1.0) * gate * _sigmoid(SWIGLU_ALPHA * gate)).astype(BF16))
        act = jnp.concatenate(acts, axis=1)
        out_ref[...] = _dot(act, wd_scr[...]) + bd_ref[0]

    @pl.when(jnp.logical_not(used))
    def _():
        out_ref[...] = jnp.zeros_like(out_ref)


def _experts(xs, block_e, n_used, slot_of, next_e, w_gu_all, b_gu, w_d_all, b_d, layer):
    n_slots, d = xs.shape
    n_blocks = n_slots // MOE_BLOCK
    de2 = w_gu_all.shape[3]
    de = w_d_all.shape[2]
    grid_spec = pltpu.PrefetchScalarGridSpec(
        num_scalar_prefetch=4,
        grid=(n_blocks,),
        in_specs=[pl.BlockSpec((MOE_BLOCK, d), lambda b, be, nu, sl, nx: (b, 0)),
                  pl.BlockSpec((1, 1, de2), lambda b, be, nu, sl, nx: (be[b], 0, 0)),
                  pl.BlockSpec((1, 1, d), lambda b, be, nu, sl, nx: (be[b], 0, 0)),
                  pl.BlockSpec(memory_space=pl.ANY),
                  pl.BlockSpec(memory_space=pl.ANY)],
        out_specs=pl.BlockSpec((MOE_BLOCK, d), lambda b, be, nu, sl, nx: (b, 0)),
        scratch_shapes=[pltpu.VMEM((2, d, de2), F32), pltpu.VMEM((2, de, d), F32),
                        pltpu.VMEM((d, de2), BF16), pltpu.VMEM((de, d), BF16),
                        pltpu.SemaphoreType.DMA((2, 2))],
    )
    return pl.pallas_call(
        functools.partial(_expert_kernel, layer=layer),
        grid_spec=grid_spec,
        out_shape=jax.ShapeDtypeStruct((n_slots, d), F32),
        compiler_params=_cparams("arbitrary"),
        name="moe_experts",
    )(block_e, n_used, slot_of, next_e, xs, b_gu, b_d, w_gu_all, w_d_all)


def _combine_kernel(dest_ref, dest_next_ref, gate_ref, h_ref, g_ref, b_ref, eo_hbm, out_ref, buf, sems,
                    *, lp):
    tm = h_ref.shape[0]
    i = pl.program_id(0)
    slot = i % 2

    def gather(idx_ref, into):
        def make_copy(t, k):
            return pltpu.make_async_copy(eo_hbm.at[pl.ds(idx_ref[t * TOP_K + k], 1)],
                                         buf.at[into, k, pl.ds(t, 1)], sems.at[into])
        return make_copy

    @pl.when(i == 0)
    def _():
        _issue_row_copies(gather(dest_ref, slot), tm)

    @pl.when(i + 1 < pl.num_programs(0))
    def _():
        _issue_row_copies(gather(dest_next_ref, 1 - slot), tm)

    _wait_row_copies(gather(dest_ref, slot), tm)

    gates = gate_ref[...]
    ffn = gates[:, 0:1] * buf[slot, 0]
    for k in range(1, TOP_K):
        ffn = ffn + gates[:, k:k + 1] * buf[slot, k]
    out_ref[...] = _res_ln(h_ref[...], ffn, g_ref[...], b_ref[...], _pad_rows_mask(tm, lp))


def _combine(eo, dest_flat, gates, h, ln_g, ln_b, tm, lp):
    r, d = h.shape
    n_tiles = r // tm
    full = lambda shape: pl.BlockSpec(shape, lambda i: (0, 0))
    return pl.pallas_call(
        functools.partial(_combine_kernel, lp=lp),
        grid=(n_tiles,),
        in_specs=[pl.BlockSpec((tm * TOP_K,), lambda i: (i,), memory_space=pltpu.SMEM),
                  pl.BlockSpec((tm * TOP_K,), lambda i: (jnp.minimum(i + 1, n_tiles - 1),),
                               memory_space=pltpu.SMEM),
                  pl.BlockSpec((tm, LANES), lambda i: (i, 0)),
                  pl.BlockSpec((tm, d), lambda i: (i, 0)),
                  full((1, d)), full((1, d)),
                  pl.BlockSpec(memory_space=pl.ANY)],
        out_specs=pl.BlockSpec((tm, d), lambda i: (i, 0)),
        out_shape=jax.ShapeDtypeStruct((r, d), F32),
        scratch_shapes=[pltpu.VMEM((2, TOP_K, tm, d), F32), pltpu.SemaphoreType.DMA((2,))],
        compiler_params=_cparams("arbitrary"),
        name="moe_combine",
    )(dest_flat, dest_flat, gates, h, ln_g, ln_b, eo)


def _moe(h, w_router, b_router, w_gu_all, b_gu, w_d_all, b_d, ln_g, ln_b, tm, lp, xs_prev, layer):
    r, d = h.shape
    w_r = jnp.pad(w_router, ((0, 0), (0, LANES - N_EXPERTS))).astype(BF16)
    b_r = jnp.pad(b_router, (0, LANES - N_EXPERTS), constant_values=NEG)[None, :]
    meta, gates, cnt = _router(h, w_r, b_r, tm)
    counts = cnt[0, :N_EXPERTS].astype(jnp.int32)
    padded = (counts + MOE_BLOCK - 1) // MOE_BLOCK * MOE_BLOCK
    ends_p = jnp.cumsum(padded)
    pstart = ends_p - padded
    n_blocks = -(-(r * TOP_K) // MOE_BLOCK) + N_EXPERTS
    n_slots = n_blocks * MOE_BLOCK
    block_start = jnp.arange(n_blocks, dtype=jnp.int32) * MOE_BLOCK
    block_e = jnp.minimum(jnp.sum((ends_p[None, :] <= block_start[:, None]).astype(jnp.int32), axis=1),
                          N_EXPERTS - 1)
    n_used = (ends_p[-1:] // MOE_BLOCK).astype(jnp.int32)
    dest = (pstart[meta[:, :TOP_K]] + meta[:, TOP_K:2 * TOP_K]).reshape(-1)
    nonempty = padded > 0
    ids = jnp.arange(N_EXPERTS, dtype=jnp.int32)
    seg = jnp.cumsum(nonempty.astype(jnp.int32)) - nonempty.astype(jnp.int32)
    later = jnp.min(jnp.where(jnp.logical_and(nonempty[None, :], ids[None, :] > ids[:, None]),
                              ids[None, :], N_EXPERTS), axis=1)
    slot_of = (seg % 2)[block_e]
    next_e = jnp.where(later == N_EXPERTS, -1, later)[block_e]
    xs = _dispatch(h, dest, jnp.zeros((n_slots, d), h.dtype) if xs_prev is None else xs_prev, tm)
    half = w_gu_all.shape[3] // 2
    b_gu_t = b_gu.reshape(N_EXPERTS, half // LANES, LANES, 2).transpose(0, 1, 3, 2)
    b_gu_t = b_gu_t.reshape(N_EXPERTS, 1, 2 * half)
    eo = _experts(xs, block_e, n_used, slot_of, next_e, w_gu_all, b_gu_t, w_d_all, b_d[:, None, :], layer)
    return _combine(eo, dest, gates, h, ln_g, ln_b, tm, lp), xs


def _pad_heads(w, heads, dim):
    k = w.shape[0]
    return jnp.pad(w.reshape(k, heads, dim), ((0, 0), (0, 0), (0, LANES - dim))).reshape(k, heads * LANES)


def _gate_weights(wg):
    k, n = wg.shape
    return (jnp.pad(wg, ((0, 0), (0, LANES - n))).astype(BF16),
            jnp.pad(wg.T, ((0, 16 - n), (0, 0))).astype(BF16))


def _even_layer(h, p, bsz, nc, tm, lp):
    r = h.shape[0]
    w_in = p['w_in']
    aq, av = A_HEADS * A_DQK, A_HEADS * A_DV
    o0 = 2 * aq + 2 * av
    w_main = jnp.concatenate([_pad_heads(w_in[:, :aq], A_HEADS, A_DQK),
                              _pad_heads(w_in[:, aq:2 * aq], A_HEADS, A_DQK),
                              w_in[:, 2 * aq:2 * aq + av], w_in[:, 2 * aq + av:o0],
                              w_in[:, o0 + 2 * A_HEADS:]], axis=1).astype(BF16)
    w_gate, w_gate_t = _gate_weights(w_in[:, o0:o0 + 2 * A_HEADS])
    y, gcol, grow = _in_proj(h, w_main, w_gate, w_gate_t, _chunk_tile(r // bsz, 704), w_main.shape[1])
    gb = p['gate_bias']
    bias_col = jnp.pad(gb, (0, LANES - gb.shape[0]))[None, :]
    bias_row = jnp.broadcast_to(jnp.pad(gb, (0, 16 - gb.shape[0]))[:, None], (16, CHUNK))
    ha = _mlstm(y, gcol, grow, bias_col, bias_row, p['head_norm'][None, :], bsz, nc)
    levels = max(1, math.ceil(math.log2(lp // S5_CHUNK)))
    ys = _s5(y, 4 * A_HEADS * LANES,
             _s5_params(p['a_re'], p['a_im'], p['log_step'], p['b_re'], p['b_im'],
                        p['c_re'], p['c_im'], p['d'], levels), bsz, lp)
    return _even_out(ha, ys, h, p['w_glu'].astype(BF16), p['b_glu'][None, :], p['w_out'].astype(BF16),
                     p['ln_g'][None, :], p['ln_b'][None, :], tm, lp)


def _odd_layer(h, p, bsz, nc, tm, lp):
    w_in = p['w_in']
    cw = C_HEADS * C_DK
    w_main = w_in[:, :4 * cw].astype(BF16)
    w_gate, w_gate_t = _gate_weights(w_in[:, 4 * cw:])
    y, gcol, grow = _in_proj(h, w_main, w_gate, w_gate_t, _chunk_tile(h.shape[0] // bsz, 704),
                             w_main.shape[1])
    neg_a = -jnp.exp(p['a_log'])
    zeros = jnp.zeros((C_HEADS,), F32)
    dt16 = jnp.concatenate([zeros, p['dt_bias']])
    na16 = jnp.concatenate([zeros, neg_a])
    pcol = jnp.pad(jnp.stack([dt16, na16]), ((0, 0), (0, LANES - 16)))
    prow = jnp.broadcast_to(jnp.stack([dt16, na16])[:, :, None], (2, 16, CHUNK))
    o = _gdn(y, gcol, grow, p['conv'], pcol, prow, p['norm'][None, :], bsz, nc)
    return _odd_out(o, h, p['w_out'].astype(BF16), p['ln_g'][None, :], p['ln_b'][None, :], tm, lp)


def kernel(x, meta_tokens, ln_g, ln_b, ev_w_in, ev_gate_bias, ev_head_norm, s5_a_re, s5_a_im, s5_log_step, s5_b_re, s5_b_im, s5_c_re, s5_c_im, s5_d, s5_w_glu, s5_b_glu, ev_w_out, od_w_in, od_conv, od_a_log, od_dt_bias, od_norm, od_w_out, moe_w_router, moe_b_router, moe_w_gate_up, moe_b_gate_up, moe_w_down, moe_b_down):
    bsz, seq, d = x.shape
    lp = PAD + N_META + seq
    assert lp % CHUNK == 0 and d == D_MODEL
    nc = lp // CHUNK
    r = bsz * lp
    assert r % MOE_BLOCK == 0 and lp >= MOE_BLOCK
    tm = 512 if (r % 512 == 0 and lp >= 512) else MOE_BLOCK
    meta = jnp.broadcast_to(meta_tokens[None], (bsz, N_META, d)).astype(x.dtype)
    h = jnp.concatenate([jnp.zeros((bsz, PAD, d), x.dtype), meta, x], axis=1).reshape(bsz * lp, d)
    xs = None
    for layer in range(ln_g.shape[0]):
        j = layer // 2
        if layer % 2 == 0:
            p = dict(w_in=ev_w_in[j], gate_bias=ev_gate_bias[j], head_norm=ev_head_norm[j],
                     a_re=s5_a_re[j], a_im=s5_a_im[j], log_step=s5_log_step[j], b_re=s5_b_re[j],
                     b_im=s5_b_im[j], c_re=s5_c_re[j], c_im=s5_c_im[j], d=s5_d[j],
                     w_glu=s5_w_glu[j], b_glu=s5_b_glu[j], w_out=ev_w_out[j],
                     ln_g=ln_g[layer, 0], ln_b=ln_b[layer, 0])
            h = _even_layer(h, p, bsz, nc, tm, lp)
        else:
            p = dict(w_in=od_w_in[j], conv=od_conv[j], a_log=od_a_log[j], dt_bias=od_dt_bias[j],
                     norm=od_norm[j], w_out=od_w_out[j], ln_g=ln_g[layer, 0], ln_b=ln_b[layer, 0])
            h = _odd_layer(h, p, bsz, nc, tm, lp)
        h, xs = _moe(h, moe_w_router[layer], moe_b_router[layer], moe_w_gate_up,
                     moe_b_gate_up[layer], moe_w_down, moe_b_down[layer],
                     ln_g[layer, 1][None, :], ln_b[layer, 1][None, :], MOE_BLOCK, lp, xs, layer)
    return h.reshape(bsz, lp, d)[:, PAD + N_META:]
```

```python
import functools
import math

import jax
import jax.numpy as jnp
from jax import lax
from jax.experimental import pallas as pl
from jax.experimental.pallas import tpu as pltpu

F32 = jnp.float32
BF16 = jnp.bfloat16

D_MODEL = 1024
DEPTH = 4
N_META = 16
CHUNK = 64
PAD = CHUNK - N_META
NEG = -1e30
LN_EPS = 1e-5
DN_ALPHA = (2.0 * DEPTH) ** 0.25

A_HEADS = 4
A_DQK = D_MODEL // 16
A_DV = D_MODEL // 8
A_GATE_CAP = 15.0
B_CH = D_MODEL // 2
B_GROUP = 16
B_GROUPS = B_CH // B_GROUP
B_STATE = 64
S5_CHUNK = 16
C_HEADS = D_MODEL // 128
C_DK = 128
C_CONV = 4
N_EXPERTS = 32
TOP_K = 4
SWIGLU_LIMIT = 7.0
SWIGLU_ALPHA = 1.702
MOE_BLOCK = 256

LANES = 128
VMEM_LIMIT = 56 * 1024 * 1024


def _cparams(*sem):
    return pltpu.CompilerParams(dimension_semantics=sem, vmem_limit_bytes=VMEM_LIMIT)


def _chunk_tile(lp, target):
    best = CHUNK
    for t in range(CHUNK, target + 1, CHUNK):
        if lp % t == 0:
            best = t
    return best


def _dot(a, b):
    return jnp.dot(a, b, preferred_element_type=F32)


def _dot_nt(a, b):
    return lax.dot_general(a, b, (((1,), (1,)), ((), ())), preferred_element_type=F32)


def _dot_tn(a, b):
    return lax.dot_general(a, b, (((0,), (0,)), ((), ())), preferred_element_type=F32)


def _split3(x):
    hi = x.astype(BF16)
    r1 = x - hi.astype(F32)
    mid = r1.astype(BF16)
    lo = (r1 - mid.astype(F32)).astype(BF16)
    return hi, mid, lo


def _dot01_left(t01, x):
    hi, mid, lo = _split3(x)
    return _dot(t01, hi) + _dot(t01, mid) + _dot(t01, lo)


def _dot01_right(x, t01):
    hi, mid, lo = _split3(x)
    return _dot(hi, t01) + _dot(mid, t01) + _dot(lo, t01)


def _row_sum_lanes(x):
    ones = jnp.ones((x.shape[1], LANES), BF16)
    hi = x.astype(BF16)
    lo = (x - hi.astype(F32)).astype(BF16)
    return _dot(hi, ones) + _dot(lo, ones)


def _sigmoid(x):
    return 1.0 / (1.0 + jnp.exp(-x))


def _softplus(x):
    return jnp.maximum(x, 0.0) + jnp.log(1.0 + jnp.exp(-jnp.abs(x)))


def _log_sigmoid(x):
    return -_softplus(-x)


def _silu(x):
    return x * _sigmoid(x)


def _gelu_tanh(x):
    c = math.sqrt(2.0 / math.pi)
    return 0.5 * x * (1.0 + jnp.tanh(c * (x + 0.044715 * (x * x * x))))


def _iota(shape, dim):
    return lax.broadcasted_iota(jnp.int32, shape, dim)


def _tri_masks(n):
    r = _iota((n, n), 0)
    c = _iota((n, n), 1)
    return r, c


def _proj_kernel(x_ref, w_ref, wg_ref, wgt_ref, y_ref, gcol_ref, grow_ref):
    xb = x_ref[...].astype(BF16)
    y_ref[...] = _dot(xb, w_ref[...])

    @pl.when(pl.program_id(1) == 0)
    def _():
        gcol_ref[...] = _dot(xb, wg_ref[...])
        grow = _dot_nt(wgt_ref[...], xb)
        for j in range(grow_ref.shape[0]):
            grow_ref[j] = grow[:, j * CHUNK:(j + 1) * CHUNK]


def _in_proj(h, w_main, w_gate, w_gate_t, tm, tn):
    r, d = h.shape
    n = w_main.shape[1]
    cpt = tm // CHUNK
    return pl.pallas_call(
        _proj_kernel,
        grid=(r // tm, n // tn),
        in_specs=[pl.BlockSpec((tm, d), lambda i, j: (i, 0)),
                  pl.BlockSpec((d, tn), lambda i, j: (0, j)),
                  pl.BlockSpec((d, LANES), lambda i, j: (0, 0)),
                  pl.BlockSpec((16, d), lambda i, j: (0, 0))],
        out_specs=[pl.BlockSpec((tm, tn), lambda i, j: (i, j)),
                   pl.BlockSpec((tm, LANES), lambda i, j: (i, 0)),
                   pl.BlockSpec((cpt, 16, CHUNK), lambda i, j: (i, 0, 0))],
        out_shape=[jax.ShapeDtypeStruct((r, n), F32),
                   jax.ShapeDtypeStruct((r, LANES), F32),
                   jax.ShapeDtypeStruct((r // CHUNK, 16, CHUNK), F32)],
        compiler_params=_cparams("parallel", "arbitrary"),
        name="in_proj",
    )(h, w_main, w_gate, w_gate_t)


def _mlstm_kernel(q_ref, k_ref, v_ref, o_ref, gcol_ref, grow_ref, bcol_ref, brow_ref, hn_ref,
                  out_ref, c_scr, m_scr):
    c = pl.program_id(1)

    @pl.when(c == 0)
    def _():
        c_scr[...] = jnp.zeros_like(c_scr)
        m_scr[...] = jnp.zeros_like(m_scr)

    n = CHUNK
    ri, ci = _tri_masks(n)
    causal = ri >= ci
    tri_l = jnp.where(causal, 1.0, 0.0).astype(BF16)
    tri_u = jnp.where(ri <= ci, 1.0, 0.0).astype(BF16)
    first = c == 0
    pad_c = jnp.logical_and(first, _iota((n, 1), 0) < PAD)
    pad_r = jnp.logical_and(first, _iota((1, n), 1) < PAD)

    i_c, b_c, i_r, b_r = [], [], [], []
    for sq in range(q_ref.shape[0]):
        gc = A_GATE_CAP * jnp.tanh((gcol_ref[sq] + bcol_ref[...]) * (1.0 / A_GATE_CAP))
        i_c.append(jnp.where(pad_c, NEG, gc))
        b_c.append(_dot01_left(tri_l, jnp.where(pad_c, 0.0, _log_sigmoid(gc))))
        gr = A_GATE_CAP * jnp.tanh((grow_ref[sq] + brow_ref[...]) * (1.0 / A_GATE_CAP))
        i_r.append(jnp.where(pad_r, NEG, gr))
        b_r.append(_dot01_right(jnp.where(pad_r, 0.0, _log_sigmoid(gr)), tri_u))

    one_col = jnp.where(_iota((n, A_DV), 1) == 0, 1.0, 0.0).astype(BF16)
    chains = [(sq, hd) for sq in range(q_ref.shape[0]) for hd in range(A_HEADS)]
    heads = range(len(chains))
    sl = [slice(hd * LANES, (hd + 1) * LANES) for _, hd in chains]
    q = [q_ref[sq, :, sl[h]].astype(BF16) for h, (sq, _) in enumerate(chains)]
    kf = [k_ref[sq, :, sl[h]] * (A_DQK ** -0.5) for h, (sq, _) in enumerate(chains)]
    vext = [jnp.concatenate([v_ref[sq, :, sl[h]].astype(BF16), one_col], axis=1)
            for h, (sq, _) in enumerate(chains)]
    bi = [b_c[sq][:, A_HEADS + hd:A_HEADS + hd + 1] for sq, hd in chains]
    ii = [i_c[sq][:, hd:hd + 1] for sq, hd in chains]
    bj = [b_r[sq][A_HEADS + hd:A_HEADS + hd + 1, :] for sq, hd in chains]
    ij = [i_r[sq][hd:hd + 1, :] for sq, hd in chains]
    b_last = [t[n - 1:n, :] for t in bi]
    a_end = [b_last[h] - bi[h] + ii[h] for h in heads]
    m_loc = [jnp.max(t, axis=0, keepdims=True) for t in a_end]
    m0 = [m_scr[h][0:1, 0:1] for h in heads]
    d_intra = [jnp.where(causal, bi[h] - bj[h] + ij[h], NEG) for h in heads]
    d_inter = [bi[h] + m0[h] for h in heads]
    m_row = [jnp.maximum(jnp.max(d_intra[h], axis=1, keepdims=True), d_inter[h]) for h in heads]
    s = [jnp.exp(d_inter[h] - m_row[h]) for h in heads]
    qk = [(_dot_nt(q[h], kf[h].astype(BF16)) * jnp.exp(d_intra[h] - m_row[h])).astype(BF16) for h in heads]
    cext = [c_scr[h] for h in heads]
    num_ext = [_dot(qk[h], vext[h]) + s[h] * _dot(q[h], cext[h].astype(BF16)) for h in heads]
    w_end = [jnp.exp(a_end[h] - m_loc[h]) for h in heads]
    d_ext = [_dot_tn((kf[h] * w_end[h]).astype(BF16), vext[h]) for h in heads]
    for h in heads:
        m_new = jnp.maximum(b_last[h] + m0[h], m_loc[h])
        c_scr[h] = jnp.exp(b_last[h] + m0[h] - m_new) * cext[h] + jnp.exp(m_loc[h] - m_new) * d_ext[h]
        m_scr[h] = jnp.broadcast_to(m_new, (8, LANES))
    hh = [num_ext[h][:, :A_DV] / jnp.maximum(jnp.abs(num_ext[h][:, A_DV:A_DV + 1]), jnp.exp(-m_row[h]))
          for h in heads]
    mu = [_row_sum_lanes(t) * (1.0 / A_DV) for t in hh]
    hc = [hh[h] - mu[h] for h in heads]
    var = [_row_sum_lanes(t * t) * (1.0 / A_DV) for t in hc]
    for h, (sq, _) in enumerate(chains):
        hnorm = hc[h] * lax.rsqrt(var[h] + 1e-6) * hn_ref[:, sl[h]]
        out_ref[sq, :, sl[h]] = _sigmoid(o_ref[sq, :, sl[h]]) * hnorm


MLSTM_SEQS_PER_STEP = 2


def _mlstm(y, gcol, grow3, bias_col, bias_row, head_norm, bsz, nc):
    r = y.shape[0]
    lp = r // bsz
    w = A_HEADS * LANES
    ns = MLSTM_SEQS_PER_STEP if bsz % MLSTM_SEQS_PER_STEP == 0 else 1
    y4 = y.reshape(bsz // ns, ns, lp, y.shape[1])
    col = lambda j: pl.BlockSpec((None, ns, CHUNK, w), lambda b, c: (b, 0, c, j))
    out = pl.pallas_call(
        _mlstm_kernel,
        grid=(bsz // ns, nc),
        in_specs=[col(0), col(1), col(2), col(3),
                  pl.BlockSpec((None, ns, CHUNK, LANES), lambda b, c: (b, 0, c, 0)),
                  pl.BlockSpec((None, ns, None, 16, CHUNK), lambda b, c: (b, 0, c, 0, 0)),
                  pl.BlockSpec((1, LANES), lambda b, c: (0, 0)),
                  pl.BlockSpec((16, CHUNK), lambda b, c: (0, 0)),
                  pl.BlockSpec((1, w), lambda b, c: (0, 0))],
        out_specs=pl.BlockSpec((None, ns, CHUNK, w), lambda b, c: (b, 0, c, 0)),
        out_shape=jax.ShapeDtypeStruct((bsz // ns, ns, lp, w), F32),
        scratch_shapes=[pltpu.VMEM((ns * A_HEADS, LANES, 2 * LANES), F32),
                        pltpu.VMEM((ns * A_HEADS, 8, LANES), F32)],
        compiler_params=_cparams("parallel", "arbitrary"),
        name="mlstm",
    )(y4, y4, y4, y4, gcol.reshape(bsz // ns, ns, lp, LANES), grow3.reshape(bsz // ns, ns, nc, 16, CHUNK),
      bias_col, bias_row, head_norm)
    return out.reshape(r, w)


def _s5_params(a_re, a_im, log_step, b_re, b_im, c_re, c_im, d_skip, levels):
    hp = lax.Precision.HIGHEST
    g, p = a_re.shape
    dt = jnp.exp(log_step)[:, None]
    mag = jnp.exp(a_re * dt)
    lb_re, lb_im = mag * jnp.cos(a_im * dt), mag * jnp.sin(a_im * dt)
    inv = 1.0 / (a_re * a_re + a_im * a_im)
    zr, zi = lb_re - 1.0, lb_im
    fr = (zr * a_re + zi * a_im) * inv
    fi = (zi * a_re - zr * a_im) * inv
    bb_re = fr[..., None] * b_re - fi[..., None] * b_im
    bb_im = fr[..., None] * b_im + fi[..., None] * b_re
    n = S5_CHUNK
    tau = jnp.arange(n + 1, dtype=F32)[:, None, None]
    pm = jnp.exp(tau * (a_re * dt))
    pr, pi = pm * jnp.cos(tau * (a_im * dt)), pm * jnp.sin(tau * (a_im * dt))
    e_re = pr[..., None] * bb_re - pi[..., None] * bb_im
    e_im = pr[..., None] * bb_im + pi[..., None] * bb_re
    kern = (jnp.einsum('gcp,tgpd->tgcd', c_re, e_re[:n], precision=hp)
            - jnp.einsum('gcp,tgpd->tgcd', c_im, e_im[:n], precision=hp))
    idx = jnp.arange(n)
    diff = idx[None, :] - idx[:, None]
    kd = jnp.where((diff >= 0)[:, :, None, None, None], kern[jnp.clip(diff, 0, n - 1)], 0.0)
    m_t = kd.transpose(2, 0, 4, 1, 3).reshape(g, n * B_GROUP, n * B_GROUP)
    w_re = e_re[n - 1 - idx].transpose(1, 0, 3, 2)
    w_im = e_im[n - 1 - idx].transpose(1, 0, 3, 2)
    w_t = jnp.concatenate([w_re, w_im], axis=-1).reshape(g, n * B_GROUP, 2 * p)
    f_re = c_re[None] * pr[1:, :, None, :] - c_im[None] * pi[1:, :, None, :]
    f_im = c_re[None] * pi[1:, :, None, :] + c_im[None] * pr[1:, :, None, :]
    v_t = jnp.concatenate([f_re.transpose(1, 3, 0, 2), -f_im.transpose(1, 3, 0, 2)], axis=1)
    v_t = v_t.reshape(g, 2 * p, n * B_GROUP)
    steps = (n * 2.0 ** jnp.arange(levels, dtype=F32))[:, None, None]
    sm = jnp.exp(steps * (a_re * dt))
    sr, si = sm * jnp.cos(steps * (a_im * dt)), sm * jnp.sin(steps * (a_im * dt))
    lam_a = jnp.concatenate([sr, sr], axis=-1)[:, :, None, :]
    lam_b = jnp.concatenate([-si, si], axis=-1)[:, :, None, :]
    d_flat = jnp.tile(d_skip, (1, n))[:, None, :]
    return m_t.astype(BF16), w_t.astype(BF16), v_t.astype(BF16), lam_a, lam_b, d_flat


S5_GROUPS_PER_STEP = LANES // B_GROUP
S5_SCAN_OFF = 128


def _s5_kernel(u_ref, mt_ref, wt_ref, vt_ref, d_ref, la_ref, lb_ref, out_ref, scan_scr, *, nch):
    n = S5_CHUNK
    gps = S5_GROUPS_PER_STEP
    rows = 2 * nch
    levels = la_ref.shape[0]
    lane_blk = _iota((rows, LANES), 1) // B_GROUP
    r = _iota((rows, 1), 0)
    r_in = jnp.where(r >= nch, r - nch, r)
    xs = [u_ref[pl.ds(s, rows, stride=n), :] for s in range(n)]

    def block_transpose(vs):
        for dist in (4, 2, 1):
            low = (lane_blk & dist) == 0
            nxt = list(vs)
            for i in range(gps):
                if i & dist == 0:
                    a, b = vs[i], vs[i + dist]
                    nxt[i] = jnp.where(low, a, pltpu.roll(b, dist * B_GROUP, 1))
                    nxt[i + dist] = jnp.where(low, pltpu.roll(a, LANES - dist * B_GROUP, 1), b)
            vs = nxt
        return vs

    u_half = [block_transpose(xs[half * gps:(half + 1) * gps]) for half in range(2)]
    ys = []
    for g in range(gps):
        u = jnp.concatenate([u_half[0][g], u_half[1][g]], axis=1)
        ub = u.astype(BF16)
        y = _dot(ub, mt_ref[g]) + d_ref[g] * u
        x = _dot(ub, wt_ref[g])
        scan_scr[g, 0:S5_SCAN_OFF, :] = jnp.zeros((S5_SCAN_OFF, LANES), F32)
        for lv in range(levels):
            sh = 1 << lv
            scan_scr[g, S5_SCAN_OFF:S5_SCAN_OFF + rows, :] = x
            prev = scan_scr[g, S5_SCAN_OFF - sh:S5_SCAN_OFF - sh + rows, :]
            prev = jnp.where(r_in >= sh, prev, 0.0)
            x = x + la_ref[lv, g] * prev + lb_ref[lv, g] * pltpu.roll(prev, B_STATE, 1)
        scan_scr[g, S5_SCAN_OFF:S5_SCAN_OFF + rows, :] = x
        x0 = scan_scr[g, S5_SCAN_OFF - 1:S5_SCAN_OFF - 1 + rows, :]
        x0 = jnp.where(r_in >= 1, x0, 0.0)
        ys.append(y + _dot(x0.astype(BF16), vt_ref[g]))
    for half in range(2):
        steps = block_transpose([yg[:, half * LANES:(half + 1) * LANES] for yg in ys])
        for sp in range(gps):
            out_ref[pl.ds(half * gps + sp, rows, stride=n), :] = steps[sp]


def _s5(y, u_col0, params, bsz, lp):
    m_t, w_t, v_t, lam_a, lam_b, d_flat = params
    r = y.shape[0]
    nch = lp // S5_CHUNK
    gps = S5_GROUPS_PER_STEP
    wide = S5_CHUNK * B_GROUP
    st = 2 * B_STATE
    levels = lam_a.shape[0]
    assert bsz % 2 == 0 and (1 << (levels - 1)) <= S5_SCAN_OFF and u_col0 % LANES == 0
    gspec = lambda *shape: pl.BlockSpec((gps,) + shape, lambda j, b: (j, 0, 0))
    lspec = pl.BlockSpec((levels, gps, 1, st), lambda j, b: (0, j, 0, 0))
    return pl.pallas_call(
        functools.partial(_s5_kernel, nch=nch),
        grid=(B_GROUPS // gps, bsz // 2),
        in_specs=[pl.BlockSpec((2 * lp, LANES), lambda j, b: (b, u_col0 // LANES + j)),
                  gspec(wide, wide), gspec(wide, st), gspec(st, wide), gspec(1, wide), lspec, lspec],
        out_specs=pl.BlockSpec((2 * lp, LANES), lambda j, b: (b, j)),
        out_shape=jax.ShapeDtypeStruct((r, B_CH), F32),
        scratch_shapes=[pltpu.VMEM((gps, S5_SCAN_OFF + 2 * nch, LANES), F32)],
        compiler_params=_cparams("parallel", "arbitrary"),
        name="s5",
    )(y, m_t, w_t, v_t, d_flat, lam_a, lam_b)


def _res_ln(h, mix, g, b, pad_rows):
    z = DN_ALPHA * h + mix
    mu = jnp.mean(z, axis=1, keepdims=True)
    zc = z - mu
    var = jnp.mean(zc * zc, axis=1, keepdims=True)
    out = zc * lax.rsqrt(var + LN_EPS) * g + b
    return jnp.where(pad_rows, 0.0, out)


def _pad_rows_mask(tm, lp):
    start = (pl.program_id(0) * tm) % lp
    pos = start + _iota((tm, 1), 0)
    pos = jnp.where(pos >= lp, pos - lp, pos)
    return pos < PAD


def _even_out_kernel(ha_ref, ys_ref, h_ref, wglu_ref, bglu_ref, wout_ref, g_ref, b_ref, out_ref,
                     *, lp):
    tm = h_ref.shape[0]
    yb = _gelu_tanh(ys_ref[...])
    hb = yb * _sigmoid(_dot(yb.astype(BF16), wglu_ref[...]) + bglu_ref[...])
    av = ha_ref.shape[1]
    mix = _dot(ha_ref[...].astype(BF16), wout_ref[:av, :]) + _dot(hb.astype(BF16), wout_ref[av:, :])
    out_ref[...] = _res_ln(h_ref[...], mix, g_ref[...], b_ref[...], _pad_rows_mask(tm, lp))


def _even_out(ha, ys, h, w_glu, b_glu, w_out, ln_g, ln_b, tm, lp):
    r, d = h.shape
    av, bc = ha.shape[1], ys.shape[1]
    full = lambda shape: pl.BlockSpec(shape, lambda i: (0, 0))
    return pl.pallas_call(
        functools.partial(_even_out_kernel, lp=lp),
        grid=(r // tm,),
        in_specs=[pl.BlockSpec((tm, av), lambda i: (i, 0)),
                  pl.BlockSpec((tm, bc), lambda i: (i, 0)),
                  pl.BlockSpec((tm, d), lambda i: (i, 0)),
                  full((bc, bc)), full((1, bc)), full((av + bc, d)), full((1, d)), full((1, d))],
        out_specs=pl.BlockSpec((tm, d), lambda i: (i, 0)),
        out_shape=jax.ShapeDtypeStruct((r, d), F32),
        compiler_params=_cparams("parallel"),
        name="even_out",
    )(ha, ys, h, w_glu, b_glu, w_out, ln_g, ln_b)


def _gdn_kernel(q_ref, k_ref, v_ref, z_ref, gcol_ref, grow_ref, conv_ref, pcol_ref, prow_ref,
                nw_ref, out_ref, s_scr, carry_scr, ext_scr):
    c = pl.program_id(1)

    @pl.when(c == 0)
    def _():
        s_scr[...] = jnp.zeros_like(s_scr)
        carry_scr[...] = jnp.zeros_like(carry_scr)

    n = CHUNK
    ri, ci = _tri_masks(n)
    incl = ri >= ci
    strict = ri > ci
    tri_l = jnp.where(incl, 1.0, 0.0).astype(BF16)
    tri_u = jnp.where(ri <= ci, 1.0, 0.0).astype(BF16)
    first = c == 0
    pad_c = jnp.logical_and(first, _iota((n, 1), 0) < PAD)
    pad_r = jnp.logical_and(first, _iota((1, n), 1) < PAD)

    gc = gcol_ref[...]
    beta_c = jnp.where(pad_c, 0.0, _sigmoid(gc))
    g_c = jnp.where(pad_c, 0.0, pcol_ref[1:2, :] * _softplus(gc + pcol_ref[0:1, :]))
    gcum_c = _dot01_left(tri_l, g_c)
    gr = grow_ref[0]
    g_r = jnp.where(pad_r, 0.0, prow_ref[1] * _softplus(gr + prow_ref[0]))
    gcum_r = _dot01_right(g_r, tri_u)

    width = q_ref.shape[1]
    ext_scr[0:8, :] = carry_scr[...]
    for j, ref in enumerate((q_ref, k_ref, v_ref)):
        ext_scr[8:8 + n, j * width:(j + 1) * width] = ref[...]
    carry_scr[...] = ext_scr[n:n + 8, :]

    def conv(col):
        acc = conv_ref[0:1, col] * ext_scr[5:5 + n, col]
        for j in range(1, C_CONV):
            acc = acc + conv_ref[j:j + 1, col] * ext_scr[5 + j:5 + j + n, col]
        return _silu(acc)

    def l2n(x, scale=1.0):
        return x * (lax.rsqrt(jnp.sum(x * x, axis=1, keepdims=True) + 1e-6) * scale)

    heads = range(C_HEADS)
    lanes = lambda base, h: slice(base + h * LANES, base + (h + 1) * LANES)
    q = [l2n(conv(lanes(0, h)), C_DK ** -0.5) for h in heads]
    k = [l2n(conv(lanes(width, h))) for h in heads]
    v = [conv(lanes(2 * width, h)) for h in heads]
    beta = [beta_c[:, h:h + 1] for h in heads]
    gi = [gcum_c[:, C_HEADS + h:C_HEADS + h + 1] for h in heads]
    gj = [gcum_r[C_HEADS + h:C_HEADS + h + 1, :] for h in heads]
    g_last = [g[n - 1:n, :] for g in gi]
    decay = [jnp.where(incl, jnp.exp(jnp.where(incl, gi[h] - gj[h], 0.0)), 0.0) for h in heads]
    eg = [jnp.exp(g) for g in gi]
    kb = [t.astype(BF16) for t in k]
    qkk = [_dot_nt(jnp.concatenate([q[h].astype(BF16), kb[h]], axis=0), kb[h]) for h in heads]
    attn = [(qkk[h][:n] * decay[h]).astype(BF16) for h in heads]
    ab = [jnp.where(strict, -(beta[h] * qkk[h][n:] * decay[h]), 0.0).astype(BF16) for h in heads]
    x = [jnp.concatenate([beta[h] * v[h], (beta[h] * eg[h]) * k[h]], axis=1) for h in heads]
    x = [x[h] + _dot(ab[h], x[h].astype(BF16)) for h in heads]
    for _ in range(5):
        ab = [_dot(t, t).astype(BF16) for t in ab]
        x = [x[h] + _dot(ab[h], x[h].astype(BF16)) for h in heads]
    s0 = [s_scr[h] for h in heads]
    q_dec = [(q[h] * eg[h]).astype(BF16) for h in heads]
    k_dec = [(k[h] * jnp.exp(g_last[h] - gi[h])).astype(BF16) for h in heads]
    ws_qs = [_dot(jnp.concatenate([x[h][:, LANES:].astype(BF16), q_dec[h]], axis=0), s0[h].astype(BF16))
             for h in heads]
    vb = [(x[h][:, :LANES] - ws_qs[h][:n]).astype(BF16) for h in heads]
    o = [ws_qs[h][n:] + _dot(attn[h], vb[h]) for h in heads]
    for h in heads:
        s_scr[h] = jnp.exp(g_last[h]) * s0[h] + _dot_tn(k_dec[h], vb[h])
    ms = [jnp.mean(t * t, axis=1, keepdims=True) for t in o]
    for h in heads:
        on = o[h] * lax.rsqrt(ms[h] + 1e-6) * nw_ref[...]
        out_ref[:, lanes(0, h)] = on * _silu(z_ref[:, lanes(0, h)])


def _gdn(y, gcol, grow3, conv_w, pcol, prow, norm_w, bsz, nc):
    r = y.shape[0]
    w = C_HEADS * C_DK
    row = lambda b, c: b * nc + c
    return pl.pallas_call(
        _gdn_kernel,
        grid=(bsz, nc),
        in_specs=[pl.BlockSpec((CHUNK, w), lambda b, c: (row(b, c), 0)),
                  pl.BlockSpec((CHUNK, w), lambda b, c: (row(b, c), 1)),
                  pl.BlockSpec((CHUNK, w), lambda b, c: (row(b, c), 2)),
                  pl.BlockSpec((CHUNK, w), lambda b, c: (row(b, c), 3)),
                  pl.BlockSpec((CHUNK, LANES), lambda b, c: (row(b, c), 0)),
                  pl.BlockSpec((1, 16, CHUNK), lambda b, c: (row(b, c), 0, 0)),
                  pl.BlockSpec((C_CONV, 3 * w), lambda b, c: (0, 0)),
                  pl.BlockSpec((2, LANES), lambda b, c: (0, 0)),
                  pl.BlockSpec((2, 16, CHUNK), lambda b, c: (0, 0, 0)),
                  pl.BlockSpec((1, LANES), lambda b, c: (0, 0))],
        out_specs=pl.BlockSpec((CHUNK, w), lambda b, c: (row(b, c), 0)),
        out_shape=jax.ShapeDtypeStruct((r, w), F32),
        scratch_shapes=[pltpu.VMEM((C_HEADS, C_DK, C_DK), F32),
                        pltpu.VMEM((8, 3 * w), F32),
                        pltpu.VMEM((CHUNK + 8, 3 * w), F32)],
        compiler_params=_cparams("parallel", "arbitrary"),
        name="gdn",
    )(y, y, y, y, gcol, grow3, conv_w, pcol, prow, norm_w)


def _odd_out_kernel(o_ref, h_ref, wout_ref, g_ref, b_ref, out_ref, *, lp):
    tm = h_ref.shape[0]
    mix = _dot(o_ref[...].astype(BF16), wout_ref[...])
    out_ref[...] = _res_ln(h_ref[...], mix, g_ref[...], b_ref[...], _pad_rows_mask(tm, lp))


def _odd_out(o, h, w_out, ln_g, ln_b, tm, lp):
    r, d = h.shape
    full = lambda shape: pl.BlockSpec(shape, lambda i: (0, 0))
    return pl.pallas_call(
        functools.partial(_odd_out_kernel, lp=lp),
        grid=(r // tm,),
        in_specs=[pl.BlockSpec((tm, d), lambda i: (i, 0)),
                  pl.BlockSpec((tm, d), lambda i: (i, 0)),
                  full((d, d)), full((1, d)), full((1, d))],
        out_specs=pl.BlockSpec((tm, d), lambda i: (i, 0)),
        out_shape=jax.ShapeDtypeStruct((r, d), F32),
        compiler_params=_cparams("parallel"),
        name="odd_out",
    )(o, h, w_out, ln_g, ln_b)


def _router_kernel(x_ref, w_ref, b_ref, meta_ref, gate_ref, cnt_ref, base_scr):
    @pl.when(pl.program_id(0) == 0)
    def _():
        base_scr[...] = jnp.zeros_like(base_scr)

    tm = x_ref.shape[0]
    logits = _dot(x_ref[...].astype(BF16), w_ref[...]) + b_ref[...]
    lane = _iota((tm, LANES), 1)
    lane_f = lane.astype(F32)
    work = logits
    vals, sels = [], []
    onehot = jnp.zeros((tm, LANES), F32)
    for _ in range(TOP_K):
        m = jnp.max(work, axis=1, keepdims=True)
        idx = jnp.min(jnp.where(work == m, lane_f, float(LANES)), axis=1, keepdims=True)
        sel = lane_f == idx
        vals.append(m)
        sels.append((sel, idx))
        onehot = onehot + jnp.where(sel, 1.0, 0.0)
        work = jnp.where(sel, -jnp.inf, work)
    ri, ci = _tri_masks(tm)
    tri = jnp.where(ri > ci, 1.0, 0.0).astype(BF16)
    before = _dot(tri, onehot.astype(BF16)) + base_scr[0:1, :]
    base_scr[...] = base_scr[...] + jnp.sum(onehot, axis=0, keepdims=True)
    cnt_ref[...] = base_scr[...]
    es = [jnp.exp(v - vals[0]) for v in vals]
    tot = es[0] + es[1] + es[2] + es[3]
    meta = jnp.zeros((tm, LANES), jnp.int32)
    gate = jnp.zeros((tm, LANES), F32)
    for k, (sel, idx) in enumerate(sels):
        rank = jnp.sum(jnp.where(sel, before, 0.0), axis=1, keepdims=True)
        meta = jnp.where(lane == k, idx.astype(jnp.int32), meta)
        meta = jnp.where(lane == TOP_K + k, rank.astype(jnp.int32), meta)
        gate = jnp.where(lane == k, es[k] / tot, gate)
    meta_ref[...] = meta
    gate_ref[...] = gate


def _router(h, w_r, b_r, tm):
    r, d = h.shape
    return pl.pallas_call(
        _router_kernel,
        grid=(r // tm,),
        in_specs=[pl.BlockSpec((tm, d), lambda i: (i, 0)),
                  pl.BlockSpec((d, LANES), lambda i: (0, 0)),
                  pl.BlockSpec((1, LANES), lambda i: (0, 0))],
        out_specs=[pl.BlockSpec((tm, LANES), lambda i: (i, 0)),
                   pl.BlockSpec((tm, LANES), lambda i: (i, 0)),
                   pl.BlockSpec((8, LANES), lambda i: (0, 0))],
        out_shape=[jax.ShapeDtypeStruct((r, LANES), jnp.int32),
                   jax.ShapeDtypeStruct((r, LANES), F32),
                   jax.ShapeDtypeStruct((8, LANES), F32)],
        scratch_shapes=[pltpu.VMEM((8, LANES), F32)],
        compiler_params=_cparams("arbitrary"),
        name="router",
    )(h, w_r, b_r)


ROW_DMA_UNROLL = 8
ROW_DMA_WAIT_GROUP = 64


def _issue_row_copies(make_copy, tm):
    def issue(i, carry):
        for u in range(ROW_DMA_UNROLL):
            for k in range(TOP_K):
                make_copy(i * ROW_DMA_UNROLL + u, k).start(priority=(u * TOP_K + k) % 2)
        return carry

    lax.fori_loop(0, tm // ROW_DMA_UNROLL, issue, 0)


def _wait_row_copies(make_copy, tm):
    def drain(i, carry):
        for _ in range(ROW_DMA_WAIT_GROUP):
            make_copy(0, 0).wait()
        return carry

    lax.fori_loop(0, tm * TOP_K // ROW_DMA_WAIT_GROUP, drain, 0)


def _dispatch_kernel(dest_ref, x_hbm, xs_in, xs_hbm, stage, sem_in, sem_out, *, tm):
    del xs_in
    i = pl.program_id(0)
    n = pl.num_programs(0)
    cur = i % 3

    def load(tile, into):
        return pltpu.make_async_copy(x_hbm.at[pl.ds(tile * tm, tm)], stage.at[into], sem_in.at[into])

    def scatter(src_slot, sem_slot):
        def make_copy(t, k):
            return pltpu.make_async_copy(stage.at[src_slot, pl.ds(t, 1)],
                                         xs_hbm.at[pl.ds(dest_ref[t * TOP_K + k], 1)], sem_out.at[sem_slot])
        return make_copy

    @pl.when(i == 0)
    def _():
        load(0, 0).start()

    @pl.when(i + 1 < n)
    def _():
        load(i + 1, (i + 1) % 3).start()

    load(i, cur).wait()
    _issue_row_copies(scatter(cur, i % 2), tm)

    @pl.when(i >= 1)
    def _():
        _wait_row_copies(scatter((i + 2) % 3, (i + 1) % 2), tm)

    @pl.when(i == n - 1)
    def _():
        _wait_row_copies(scatter(cur, i % 2), tm)


def _dispatch(h, dest_flat, xs0, tm):
    r, d = h.shape
    n_slots = xs0.shape[0]
    return pl.pallas_call(
        functools.partial(_dispatch_kernel, tm=tm),
        grid=(r // tm,),
        in_specs=[pl.BlockSpec((tm * TOP_K,), lambda i: (i,), memory_space=pltpu.SMEM),
                  pl.BlockSpec(memory_space=pl.ANY),
                  pl.BlockSpec(memory_space=pl.ANY)],
        out_specs=pl.BlockSpec(memory_space=pl.ANY),
        out_shape=jax.ShapeDtypeStruct((n_slots, d), h.dtype),
        scratch_shapes=[pltpu.VMEM((3, tm, d), h.dtype), pltpu.SemaphoreType.DMA((3,)),
                        pltpu.SemaphoreType.DMA((2,))],
        input_output_aliases={2: 0},
        compiler_params=_cparams("arbitrary"),
        name="moe_dispatch",
    )(dest_flat, h, xs0)


EXPERT_BLOCKS_PER_STEP = 2


def _expert_kernel(be_ref, nu_ref, slot_ref, nxt_ref, x_ref, bgu_ref, bd_ref, wgu_hbm, wd_hbm, out_ref,
                   wgu_buf, wd_buf, wgu_scr, wd_scr, sems, *, layer):
    tile = 2 * LANES
    n_tiles = wgu_buf.shape[2] // tile

    def weight_copies(expert, into):
        return (pltpu.make_async_copy(wgu_hbm.at[layer, expert], wgu_buf.at[into], sems.at[0, into]),
                pltpu.make_async_copy(wd_hbm.at[layer, expert], wd_buf.at[into], sems.at[1, into]))

    def one_block(b, rows):
        used = b < nu_ref[0]
        e = be_ref[b]
        fresh = jnp.logical_and(used, jnp.logical_or(b == 0, e != be_ref[jnp.maximum(b - 1, 0)]))
        slot = slot_ref[b]

        @pl.when(jnp.logical_and(used, b == 0))
        def _():
            for cp in weight_copies(e, slot):
                cp.start()

        @pl.when(fresh)
        def _():
            for cp in weight_copies(e, slot):
                cp.wait()

            @pl.when(nxt_ref[b] >= 0)
            def _():
                for cp in weight_copies(nxt_ref[b], 1 - slot):
                    cp.start()

            r = _iota((tile, tile), 0)
            c = _iota((tile, tile), 1)
            src = jnp.where(c < LANES, 2 * c, 2 * (c - LANES) + 1)
            perm = jnp.where(r == src, 1.0, 0.0).astype(BF16)
            for j in range(n_tiles):
                cols = slice(j * tile, (j + 1) * tile)
                wgu_scr[:, cols] = _dot(wgu_buf[slot, :, cols].astype(BF16), perm).astype(BF16)
            wd_scr[...] = wd_buf[slot].astype(BF16)

        @pl.when(used)
        def _():
            xb = x_ref[rows, :].astype(BF16)
            h = _dot(xb, wgu_scr[...]) + bgu_ref[e]
            acts = []
            for j in range(n_tiles):
                gate = jnp.minimum(h[:, j * tile:j * tile + LANES], SWIGLU_LIMIT)
                up = jnp.clip(h[:, j * tile + LANES:(j + 1) * tile], -SWIGLU_LIMIT, SWIGLU_LIMIT)
                acts.append(((up + 1.0) * gate * _sigmoid(SWIGLU_ALPHA * gate)).astype(BF16))
            act = jnp.concatenate(acts, axis=1)
            out_ref[rows, :] = _dot(act, wd_scr[...]) + bd_ref[e]

        @pl.when(jnp.logical_not(used))
        def _():
            out_ref[rows, :] = jnp.zeros((MOE_BLOCK, out_ref.shape[1]), out_ref.dtype)

    for s in range(EXPERT_BLOCKS_PER_STEP):
        one_block(pl.program_id(0) * EXPERT_BLOCKS_PER_STEP + s, slice(s * MOE_BLOCK, (s + 1) * MOE_BLOCK))


def _experts(xs, block_e, n_used, slot_of, next_e, w_gu_all, b_gu, w_d_all, b_d, layer):
    n_slots, d = xs.shape
    n_blocks = n_slots // MOE_BLOCK
    de2 = w_gu_all.shape[3]
    de = w_d_all.shape[2]
    rows = EXPERT_BLOCKS_PER_STEP * MOE_BLOCK
    assert n_blocks % EXPERT_BLOCKS_PER_STEP == 0
    grid_spec = pltpu.PrefetchScalarGridSpec(
        num_scalar_prefetch=4,
        grid=(n_blocks // EXPERT_BLOCKS_PER_STEP,),
        in_specs=[pl.BlockSpec((rows, d), lambda i, be, nu, sl, nx: (i, 0)),
                  pl.BlockSpec((N_EXPERTS, 1, de2), lambda i, be, nu, sl, nx: (0, 0, 0)),
                  pl.BlockSpec((N_EXPERTS, 1, d), lambda i, be, nu, sl, nx: (0, 0, 0)),
                  pl.BlockSpec(memory_space=pl.ANY),
                  pl.BlockSpec(memory_space=pl.ANY)],
        out_specs=pl.BlockSpec((rows, d), lambda i, be, nu, sl, nx: (i, 0)),
        scratch_shapes=[pltpu.VMEM((2, d, de2), F32), pltpu.VMEM((2, de, d), F32),
                        pltpu.VMEM((d, de2), BF16), pltpu.VMEM((de, d), BF16),
                        pltpu.SemaphoreType.DMA((2, 2))],
    )
    return pl.pallas_call(
        functools.partial(_expert_kernel, layer=layer),
        grid_spec=grid_spec,
        out_shape=jax.ShapeDtypeStruct((n_slots, d), F32),
        compiler_params=_cparams("arbitrary"),
        name="moe_experts",
    )(block_e, n_used, slot_of, next_e, xs, b_gu, b_d, w_gu_all, w_d_all)


def _combine_kernel(dest_ref, dest_next_ref, gate_ref, h_ref, g_ref, b_ref, eo_hbm, out_ref, buf, sems,
                    *, lp):
    tm = h_ref.shape[0]
    i = pl.program_id(0)
    slot = i % 2

    def gather(idx_ref, into):
        def make_copy(t, k):
            return pltpu.make_async_copy(eo_hbm.at[pl.ds(idx_ref[t * TOP_K + k], 1)],
                                         buf.at[into, k, pl.ds(t, 1)], sems.at[into])
        return make_copy

    @pl.when(i == 0)
    def _():
        _issue_row_copies(gather(dest_ref, slot), tm)

    @pl.when(i + 1 < pl.num_programs(0))
    def _():
        _issue_row_copies(gather(dest_next_ref, 1 - slot), tm)

    _wait_row_copies(gather(dest_ref, slot), tm)

    gates = gate_ref[...]
    ffn = gates[:, 0:1] * buf[slot, 0]
    for k in range(1, TOP_K):
        ffn = ffn + gates[:, k:k + 1] * buf[slot, k]
    out_ref[...] = _res_ln(h_ref[...], ffn, g_ref[...], b_ref[...], _pad_rows_mask(tm, lp))


def _combine(eo, dest_flat, gates, h, ln_g, ln_b, tm, lp):
    r, d = h.shape
    n_tiles = r // tm
    full = lambda shape: pl.BlockSpec(shape, lambda i: (0, 0))
    return pl.pallas_call(
        functools.partial(_combine_kernel, lp=lp),
        grid=(n_tiles,),
        in_specs=[pl.BlockSpec((tm * TOP_K,), lambda i: (i,), memory_space=pltpu.SMEM),
                  pl.BlockSpec((tm * TOP_K,), lambda i: (jnp.minimum(i + 1, n_tiles - 1),),
                               memory_space=pltpu.SMEM),
                  pl.BlockSpec((tm, LANES), lambda i: (i, 0)),
                  pl.BlockSpec((tm, d), lambda i: (i, 0)),
                  full((1, d)), full((1, d)),
                  pl.BlockSpec(memory_space=pl.ANY)],
        out_specs=pl.BlockSpec((tm, d), lambda i: (i, 0)),
        out_shape=jax.ShapeDtypeStruct((r, d), F32),
        scratch_shapes=[pltpu.VMEM((2, TOP_K, tm, d), F32), pltpu.SemaphoreType.DMA((2,))],
        compiler_params=_cparams("arbitrary"),
        name="moe_combine",
    )(dest_flat, dest_flat, gates, h, ln_g, ln_b, eo)


def _moe(h, w_router, b_router, w_gu_all, b_gu, w_d_all, b_d, ln_g, ln_b, tm, lp, xs_prev, layer):
    r, d = h.shape
    w_r = jnp.pad(w_router, ((0, 0), (0, LANES - N_EXPERTS))).astype(BF16)
    b_r = jnp.pad(b_router, (0, LANES - N_EXPERTS), constant_values=NEG)[None, :]
    meta, gates, cnt = _router(h, w_r, b_r, 2 * tm if r % (2 * tm) == 0 else tm)
    counts = cnt[0, :N_EXPERTS].astype(jnp.int32)
    padded = (counts + MOE_BLOCK - 1) // MOE_BLOCK * MOE_BLOCK
    ends_p = jnp.cumsum(padded)
    pstart = ends_p - padded
    n_blocks = -(-(r * TOP_K) // MOE_BLOCK) + N_EXPERTS
    n_slots = n_blocks * MOE_BLOCK
    block_start = jnp.arange(n_blocks, dtype=jnp.int32) * MOE_BLOCK
    block_e = jnp.minimum(jnp.sum((ends_p[None, :] <= block_start[:, None]).astype(jnp.int32), axis=1),
                          N_EXPERTS - 1)
    n_used = (ends_p[-1:] // MOE_BLOCK).astype(jnp.int32)
    dest = (pstart[meta[:, :TOP_K]] + meta[:, TOP_K:2 * TOP_K]).reshape(-1)
    nonempty = padded > 0
    ids = jnp.arange(N_EXPERTS, dtype=jnp.int32)
    seg = jnp.cumsum(nonempty.astype(jnp.int32)) - nonempty.astype(jnp.int32)
    later = jnp.min(jnp.where(jnp.logical_and(nonempty[None, :], ids[None, :] > ids[:, None]),
                              ids[None, :], N_EXPERTS), axis=1)
    slot_of = (seg % 2)[block_e]
    next_e = jnp.where(later == N_EXPERTS, -1, later)[block_e]
    xs = _dispatch(h, dest, jnp.zeros((n_slots, d), h.dtype) if xs_prev is None else xs_prev, tm)
    half = w_gu_all.shape[3] // 2
    b_gu_t = b_gu.reshape(N_EXPERTS, half // LANES, LANES, 2).transpose(0, 1, 3, 2)
    b_gu_t = b_gu_t.reshape(N_EXPERTS, 1, 2 * half)
    eo = _experts(xs, block_e, n_used, slot_of, next_e, w_gu_all, b_gu_t, w_d_all, b_d[:, None, :], layer)
    return _combine(eo, dest, gates, h, ln_g, ln_b, tm, lp), xs


def _pad_heads(w, heads, dim):
    k = w.shape[0]
    return jnp.pad(w.reshape(k, heads, dim), ((0, 0), (0, 0), (0, LANES - dim))).reshape(k, heads * LANES)


def _gate_weights(wg):
    k, n = wg.shape
    return (jnp.pad(wg, ((0, 0), (0, LANES - n))).astype(BF16),
            jnp.pad(wg.T, ((0, 16 - n), (0, 0))).astype(BF16))


def _even_layer(h, p, bsz, nc, tm, lp):
    r = h.shape[0]
    w_in = p['w_in']
    aq, av = A_HEADS * A_DQK, A_HEADS * A_DV
    o0 = 2 * aq + 2 * av
    w_main = jnp.concatenate([_pad_heads(w_in[:, :aq], A_HEADS, A_DQK),
                              _pad_heads(w_in[:, aq:2 * aq], A_HEADS, A_DQK),
                              w_in[:, 2 * aq:2 * aq + av], w_in[:, 2 * aq + av:o0],
                              w_in[:, o0 + 2 * A_HEADS:]], axis=1).astype(BF16)
    w_gate, w_gate_t = _gate_weights(w_in[:, o0:o0 + 2 * A_HEADS])
    y, gcol, grow = _in_proj(h, w_main, w_gate, w_gate_t, _chunk_tile(r // bsz, 704), w_main.shape[1])
    gb = p['gate_bias']
    bias_col = jnp.pad(gb, (0, LANES - gb.shape[0]))[None, :]
    bias_row = jnp.broadcast_to(jnp.pad(gb, (0, 16 - gb.shape[0]))[:, None], (16, CHUNK))
    ha = _mlstm(y, gcol, grow, bias_col, bias_row, p['head_norm'][None, :], bsz, nc)
    levels = max(1, math.ceil(math.log2(lp // S5_CHUNK)))
    ys = _s5(y, 4 * A_HEADS * LANES,
             _s5_params(p['a_re'], p['a_im'], p['log_step'], p['b_re'], p['b_im'],
                        p['c_re'], p['c_im'], p['d'], levels), bsz, lp)
    return _even_out(ha, ys, h, p['w_glu'].astype(BF16), p['b_glu'][None, :], p['w_out'].astype(BF16),
                     p['ln_g'][None, :], p['ln_b'][None, :], tm, lp)


def _odd_layer(h, p, bsz, nc, tm, lp):
    w_in = p['w_in']
    cw = C_HEADS * C_DK
    w_main = w_in[:, :4 * cw].astype(BF16)
    w_gate, w_gate_t = _gate_weights(w_in[:, 4 * cw:])
    y, gcol, grow = _in_proj(h, w_main, w_gate, w_gate_t, _chunk_tile(h.shape[0] // bsz, 704),
                             w_main.shape[1])
    neg_a = -jnp.exp(p['a_log'])
    zeros = jnp.zeros((C_HEADS,), F32)
    dt16 = jnp.concatenate([zeros, p['dt_bias']])
    na16 = jnp.concatenate([zeros, neg_a])
    pcol = jnp.pad(jnp.stack([dt16, na16]), ((0, 0), (0, LANES - 16)))
    prow = jnp.broadcast_to(jnp.stack([dt16, na16])[:, :, None], (2, 16, CHUNK))
    o = _gdn(y, gcol, grow, p['conv'], pcol, prow, p['norm'][None, :], bsz, nc)
    return _odd_out(o, h, p['w_out'].astype(BF16), p['ln_g'][None, :], p['ln_b'][None, :], tm, lp)


def kernel(x, meta_tokens, ln_g, ln_b, ev_w_in, ev_gate_bias, ev_head_norm, s5_a_re, s5_a_im, s5_log_step, s5_b_re, s5_b_im, s5_c_re, s5_c_im, s5_d, s5_w_glu, s5_b_glu, ev_w_out, od_w_in, od_conv, od_a_log, od_dt_bias, od_norm, od_w_out, moe_w_router, moe_b_router, moe_w_gate_up, moe_b_gate_up, moe_w_down, moe_b_down):
    bsz, seq, d = x.shape
    lp = PAD + N_META + seq
    assert lp % CHUNK == 0 and d == D_MODEL
    nc = lp // CHUNK
    r = bsz * lp
    assert r % MOE_BLOCK == 0 and lp >= MOE_BLOCK
    tm = 512 if (r % 512 == 0 and lp >= 512) else MOE_BLOCK
    meta = jnp.broadcast_to(meta_tokens[None], (bsz, N_META, d)).astype(x.dtype)
    h = jnp.concatenate([jnp.zeros((bsz, PAD, d), x.dtype), meta, x], axis=1).reshape(bsz * lp, d)
    xs = None
    for layer in range(ln_g.shape[0]):
        j = layer // 2
        if layer % 2 == 0:
            p = dict(w_in=ev_w_in[j], gate_bias=ev_gate_bias[j], head_norm=ev_head_norm[j],
                     a_re=s5_a_re[j], a_im=s5_a_im[j], log_step=s5_log_step[j], b_re=s5_b_re[j],
                     b_im=s5_b_im[j], c_re=s5_c_re[j], c_im=s5_c_im[j], d=s5_d[j],
                     w_glu=s5_w_glu[j], b_glu=s5_b_glu[j], w_out=ev_w_out[j],
                     ln_g=ln_g[layer, 0], ln_b=ln_b[layer, 0])
            h = _even_layer(h, p, bsz, nc, tm, lp)
        else:
            p = dict(w_in=od_w_in[j], conv=od_conv[j], a_log=od_a_log[j], dt_bias=od_dt_bias[j],
                     norm=od_norm[j], w_out=od_w_out[j], ln_g=ln_g[layer, 0], ln_b=ln_b[layer, 0])
            h = _odd_layer(h, p, bsz, nc, tm, lp)
        h, xs = _moe(h, moe_w_router[layer], moe_b_router[layer], moe_w_gate_up,
                     moe_b_gate_up[layer], moe_w_down, moe_b_down[layer],
                     ln_g[layer, 1][None, :], ln_b[layer, 1][None, :], MOE_BLOCK, lp, xs, layer)
    return h.reshape(bsz, lp, d)[:, PAD + N_META:]
```

```python
import functools
import math

import jax
import jax.numpy as jnp
from jax import lax
from jax.experimental import pallas as pl
from jax.experimental.pallas import tpu as pltpu

F32 = jnp.float32
BF16 = jnp.bfloat16

D_MODEL = 1024
DEPTH = 4
N_META = 16
CHUNK = 64
PAD = CHUNK - N_META
NEG = -1e30
LN_EPS = 1e-5
DN_ALPHA = (2.0 * DEPTH) ** 0.25

A_HEADS = 4
A_DQK = D_MODEL // 16
A_DV = D_MODEL // 8
A_GATE_CAP = 15.0
B_CH = D_MODEL // 2
B_GROUP = 16
B_GROUPS = B_CH // B_GROUP
B_STATE = 64
S5_CHUNK = 16
C_HEADS = D_MODEL // 128
C_DK = 128
C_CONV = 4
N_EXPERTS = 32
TOP_K = 4
SWIGLU_LIMIT = 7.0
SWIGLU_ALPHA = 1.702
MOE_BLOCK = 256

LANES = 128
VMEM_LIMIT = 56 * 1024 * 1024


def _cparams(*sem):
    return pltpu.CompilerParams(dimension_semantics=sem, vmem_limit_bytes=VMEM_LIMIT)


def _chunk_tile(lp, target):
    best = CHUNK
    for t in range(CHUNK, target + 1, CHUNK):
        if lp % t == 0:
            best = t
    return best


def _dot(a, b):
    return jnp.dot(a, b, preferred_element_type=F32)


def _dot_nt(a, b):
    return lax.dot_general(a, b, (((1,), (1,)), ((), ())), preferred_element_type=F32)


def _dot_tn(a, b):
    return lax.dot_general(a, b, (((0,), (0,)), ((), ())), preferred_element_type=F32)


def _split3(x):
    hi = x.astype(BF16)
    r1 = x - hi.astype(F32)
    mid = r1.astype(BF16)
    lo = (r1 - mid.astype(F32)).astype(BF16)
    return hi, mid, lo


def _dot01_left(t01, x):
    hi, mid, lo = _split3(x)
    return _dot(t01, hi) + _dot(t01, mid) + _dot(t01, lo)


def _dot01_right(x, t01):
    hi, mid, lo = _split3(x)
    return _dot(hi, t01) + _dot(mid, t01) + _dot(lo, t01)


def _row_sum_lanes(x):
    ones = jnp.ones((x.shape[1], LANES), BF16)
    hi = x.astype(BF16)
    lo = (x - hi.astype(F32)).astype(BF16)
    return _dot(hi, ones) + _dot(lo, ones)


def _sigmoid(x):
    return 1.0 / (1.0 + jnp.exp(-x))


def _softplus(x):
    return jnp.maximum(x, 0.0) + jnp.log(1.0 + jnp.exp(-jnp.abs(x)))


def _log_sigmoid(x):
    return -_softplus(-x)


def _silu(x):
    return x * _sigmoid(x)


def _gelu_tanh(x):
    c = math.sqrt(2.0 / math.pi)
    return 0.5 * x * (1.0 + jnp.tanh(c * (x + 0.044715 * (x * x * x))))


def _iota(shape, dim):
    return lax.broadcasted_iota(jnp.int32, shape, dim)


def _tri_masks(n):
    r = _iota((n, n), 0)
    c = _iota((n, n), 1)
    return r, c


def _proj_kernel(x_ref, w_ref, wg_ref, wgt_ref, y_ref, gcol_ref, grow_ref):
    xb = x_ref[...].astype(BF16)
    y_ref[...] = _dot(xb, w_ref[...])

    @pl.when(pl.program_id(1) == 0)
    def _():
        gcol_ref[...] = _dot(xb, wg_ref[...])
        grow = _dot_nt(wgt_ref[...], xb)
        for j in range(grow_ref.shape[0]):
            grow_ref[j] = grow[:, j * CHUNK:(j + 1) * CHUNK]


def _in_proj(h, w_main, w_gate, w_gate_t, tm, tn):
    r, d = h.shape
    n = w_main.shape[1]
    cpt = tm // CHUNK
    return pl.pallas_call(
        _proj_kernel,
        grid=(r // tm, n // tn),
        in_specs=[pl.BlockSpec((tm, d), lambda i, j: (i, 0)),
                  pl.BlockSpec((d, tn), lambda i, j: (0, j)),
                  pl.BlockSpec((d, LANES), lambda i, j: (0, 0)),
                  pl.BlockSpec((16, d), lambda i, j: (0, 0))],
        out_specs=[pl.BlockSpec((tm, tn), lambda i, j: (i, j)),
                   pl.BlockSpec((tm, LANES), lambda i, j: (i, 0)),
                   pl.BlockSpec((cpt, 16, CHUNK), lambda i, j: (i, 0, 0))],
        out_shape=[jax.ShapeDtypeStruct((r, n), F32),
                   jax.ShapeDtypeStruct((r, LANES), F32),
                   jax.ShapeDtypeStruct((r // CHUNK, 16, CHUNK), F32)],
        compiler_params=_cparams("parallel", "arbitrary"),
        name="in_proj",
    )(h, w_main, w_gate, w_gate_t)


def _mlstm_kernel(q_ref, k_ref, v_ref, o_ref, gcol_ref, grow_ref, bcol_ref, brow_ref, hn_ref,
                  out_ref, c_scr, m_scr):
    c = pl.program_id(1)

    @pl.when(c == 0)
    def _():
        c_scr[...] = jnp.zeros_like(c_scr)
        m_scr[...] = jnp.zeros_like(m_scr)

    n = CHUNK
    ri, ci = _tri_masks(n)
    causal = ri >= ci
    tri_l = jnp.where(causal, 1.0, 0.0).astype(BF16)
    tri_u = jnp.where(ri <= ci, 1.0, 0.0).astype(BF16)
    first = c == 0
    pad_c = jnp.logical_and(first, _iota((n, 1), 0) < PAD)
    pad_r = jnp.logical_and(first, _iota((1, n), 1) < PAD)

    i_c, b_c, i_r, b_r = [], [], [], []
    for sq in range(q_ref.shape[0]):
        gc = A_GATE_CAP * jnp.tanh((gcol_ref[sq] + bcol_ref[...]) * (1.0 / A_GATE_CAP))
        i_c.append(jnp.where(pad_c, NEG, gc))
        b_c.append(_dot01_left(tri_l, jnp.where(pad_c, 0.0, _log_sigmoid(gc))))
        gr = A_GATE_CAP * jnp.tanh((grow_ref[sq] + brow_ref[...]) * (1.0 / A_GATE_CAP))
        i_r.append(jnp.where(pad_r, NEG, gr))
        b_r.append(_dot01_right(jnp.where(pad_r, 0.0, _log_sigmoid(gr)), tri_u))

    one_col = jnp.where(_iota((n, A_DV), 1) == 0, 1.0, 0.0).astype(BF16)
    chains = [(sq, hd) for sq in range(q_ref.shape[0]) for hd in range(A_HEADS)]
    heads = range(len(chains))
    sl = [slice(hd * LANES, (hd + 1) * LANES) for _, hd in chains]
    q = [q_ref[sq, :, sl[h]].astype(BF16) for h, (sq, _) in enumerate(chains)]
    kf = [k_ref[sq, :, sl[h]] * (A_DQK ** -0.5) for h, (sq, _) in enumerate(chains)]
    vext = [jnp.concatenate([v_ref[sq, :, sl[h]].astype(BF16), one_col], axis=1)
            for h, (sq, _) in enumerate(chains)]
    bi = [b_c[sq][:, A_HEADS + hd:A_HEADS + hd + 1] for sq, hd in chains]
    ii = [i_c[sq][:, hd:hd + 1] for sq, hd in chains]
    bj = [b_r[sq][A_HEADS + hd:A_HEADS + hd + 1, :] for sq, hd in chains]
    ij = [i_r[sq][hd:hd + 1, :] for sq, hd in chains]
    b_last = [t[n - 1:n, :] for t in bi]
    a_end = [b_last[h] - bi[h] + ii[h] for h in heads]
    m_loc = [jnp.max(t, axis=0, keepdims=True) for t in a_end]
    m0 = [m_scr[h][0:1, 0:1] for h in heads]
    d_intra = [jnp.where(causal, bi[h] - bj[h] + ij[h], NEG) for h in heads]
    d_inter = [bi[h] + m0[h] for h in heads]
    m_row = [jnp.maximum(jnp.max(d_intra[h], axis=1, keepdims=True), d_inter[h]) for h in heads]
    s = [jnp.exp(d_inter[h] - m_row[h]) for h in heads]
    qk = [(_dot_nt(q[h], kf[h].astype(BF16)) * jnp.exp(d_intra[h] - m_row[h])).astype(BF16) for h in heads]
    cext = [c_scr[h] for h in heads]
    num_ext = [_dot(qk[h], vext[h]) + s[h] * _dot(q[h], cext[h].astype(BF16)) for h in heads]
    w_end = [jnp.exp(a_end[h] - m_loc[h]) for h in heads]
    d_ext = [_dot_tn((kf[h] * w_end[h]).astype(BF16), vext[h]) for h in heads]
    for h in heads:
        m_new = jnp.maximum(b_last[h] + m0[h], m_loc[h])
        c_scr[h] = jnp.exp(b_last[h] + m0[h] - m_new) * cext[h] + jnp.exp(m_loc[h] - m_new) * d_ext[h]
        m_scr[h] = jnp.broadcast_to(m_new, (8, LANES))
    hh = [num_ext[h][:, :A_DV] / jnp.maximum(jnp.abs(num_ext[h][:, A_DV:A_DV + 1]), jnp.exp(-m_row[h]))
          for h in heads]
    mu = [_row_sum_lanes(t) * (1.0 / A_DV) for t in hh]
    hc = [hh[h] - mu[h] for h in heads]
    var = [_row_sum_lanes(t * t) * (1.0 / A_DV) for t in hc]
    for h, (sq, _) in enumerate(chains):
        hnorm = hc[h] * lax.rsqrt(var[h] + 1e-6) * hn_ref[:, sl[h]]
        out_ref[sq, :, sl[h]] = _sigmoid(o_ref[sq, :, sl[h]]) * hnorm


MLSTM_SEQS_PER_STEP = 2


def _mlstm(y, gcol, grow3, bias_col, bias_row, head_norm, bsz, nc):
    r = y.shape[0]
    lp = r // bsz
    w = A_HEADS * LANES
    ns = MLSTM_SEQS_PER_STEP if bsz % MLSTM_SEQS_PER_STEP == 0 else 1
    y4 = y.reshape(bsz // ns, ns, lp, y.shape[1])
    col = lambda j: pl.BlockSpec((None, ns, CHUNK, w), lambda b, c: (b, 0, c, j))
    out = pl.pallas_call(
        _mlstm_kernel,
        grid=(bsz // ns, nc),
        in_specs=[col(0), col(1), col(2), col(3),
                  pl.BlockSpec((None, ns, CHUNK, LANES), lambda b, c: (b, 0, c, 0)),
                  pl.BlockSpec((None, ns, None, 16, CHUNK), lambda b, c: (b, 0, c, 0, 0)),
                  pl.BlockSpec((1, LANES), lambda b, c: (0, 0)),
                  pl.BlockSpec((16, CHUNK), lambda b, c: (0, 0)),
                  pl.BlockSpec((1, w), lambda b, c: (0, 0))],
        out_specs=pl.BlockSpec((None, ns, CHUNK, w), lambda b, c: (b, 0, c, 0)),
        out_shape=jax.ShapeDtypeStruct((bsz // ns, ns, lp, w), F32),
        scratch_shapes=[pltpu.VMEM((ns * A_HEADS, LANES, 2 * LANES), F32),
                        pltpu.VMEM((ns * A_HEADS, 8, LANES), F32)],
        compiler_params=_cparams("parallel", "arbitrary"),
        name="mlstm",
    )(y4, y4, y4, y4, gcol.reshape(bsz // ns, ns, lp, LANES), grow3.reshape(bsz // ns, ns, nc, 16, CHUNK),
      bias_col, bias_row, head_norm)
    return out.reshape(r, w)


def _s5_params(a_re, a_im, log_step, b_re, b_im, c_re, c_im, d_skip, levels):
    hp = lax.Precision.HIGHEST
    g, p = a_re.shape
    dt = jnp.exp(log_step)[:, None]
    mag = jnp.exp(a_re * dt)
    lb_re, lb_im = mag * jnp.cos(a_im * dt), mag * jnp.sin(a_im * dt)
    inv = 1.0 / (a_re * a_re + a_im * a_im)
    zr, zi = lb_re - 1.0, lb_im
    fr = (zr * a_re + zi * a_im) * inv
    fi = (zi * a_re - zr * a_im) * inv
    bb_re = fr[..., None] * b_re - fi[..., None] * b_im
    bb_im = fr[..., None] * b_im + fi[..., None] * b_re
    n = S5_CHUNK
    tau = jnp.arange(n + 1, dtype=F32)[:, None, None]
    pm = jnp.exp(tau * (a_re * dt))
    pr, pi = pm * jnp.cos(tau * (a_im * dt)), pm * jnp.sin(tau * (a_im * dt))
    e_re = pr[..., None] * bb_re - pi[..., None] * bb_im
    e_im = pr[..., None] * bb_im + pi[..., None] * bb_re
    kern = (jnp.einsum('gcp,tgpd->tgcd', c_re, e_re[:n], precision=hp)
            - jnp.einsum('gcp,tgpd->tgcd', c_im, e_im[:n], precision=hp))
    idx = jnp.arange(n)
    diff = idx[None, :] - idx[:, None]
    kd = jnp.where((diff >= 0)[:, :, None, None, None], kern[jnp.clip(diff, 0, n - 1)], 0.0)
    m_t = kd.transpose(2, 0, 4, 1, 3).reshape(g, n * B_GROUP, n * B_GROUP)
    w_re = e_re[n - 1 - idx].transpose(1, 0, 3, 2)
    w_im = e_im[n - 1 - idx].transpose(1, 0, 3, 2)
    w_t = jnp.concatenate([w_re, w_im], axis=-1).reshape(g, n * B_GROUP, 2 * p)
    f_re = c_re[None] * pr[1:, :, None, :] - c_im[None] * pi[1:, :, None, :]
    f_im = c_re[None] * pi[1:, :, None, :] + c_im[None] * pr[1:, :, None, :]
    v_t = jnp.concatenate([f_re.transpose(1, 3, 0, 2), -f_im.transpose(1, 3, 0, 2)], axis=1)
    v_t = v_t.reshape(g, 2 * p, n * B_GROUP)
    steps = (n * 2.0 ** jnp.arange(levels, dtype=F32))[:, None, None]
    sm = jnp.exp(steps * (a_re * dt))
    sr, si = sm * jnp.cos(steps * (a_im * dt)), sm * jnp.sin(steps * (a_im * dt))
    lam_a = jnp.concatenate([sr, sr], axis=-1)[:, :, None, :]
    lam_b = jnp.concatenate([-si, si], axis=-1)[:, :, None, :]
    d_flat = jnp.tile(d_skip, (1, n))[:, None, :]
    return m_t.astype(BF16), w_t.astype(BF16), v_t.astype(BF16), lam_a, lam_b, d_flat


S5_GROUPS_PER_STEP = LANES // B_GROUP
S5_SCAN_OFF = 128


def _s5_kernel(u_ref, mt_ref, wt_ref, vt_ref, d_ref, la_ref, lb_ref, out_ref, scan_scr, *, nch):
    n = S5_CHUNK
    gps = S5_GROUPS_PER_STEP
    rows = 2 * nch
    levels = la_ref.shape[0]
    lane_blk = _iota((rows, LANES), 1) // B_GROUP
    r = _iota((rows, 1), 0)
    r_in = jnp.where(r >= nch, r - nch, r)
    xs = [u_ref[pl.ds(s, rows, stride=n), :] for s in range(n)]

    def block_transpose(vs):
        for dist in (4, 2, 1):
            low = (lane_blk & dist) == 0
            nxt = list(vs)
            for i in range(gps):
                if i & dist == 0:
                    a, b = vs[i], vs[i + dist]
                    nxt[i] = jnp.where(low, a, pltpu.roll(b, dist * B_GROUP, 1))
                    nxt[i + dist] = jnp.where(low, pltpu.roll(a, LANES - dist * B_GROUP, 1), b)
            vs = nxt
        return vs

    u_half = [block_transpose(xs[half * gps:(half + 1) * gps]) for half in range(2)]
    ys = []
    for g in range(gps):
        u = jnp.concatenate([u_half[0][g], u_half[1][g]], axis=1)
        ub = u.astype(BF16)
        y = _dot(ub, mt_ref[g]) + d_ref[g] * u
        x = _dot(ub, wt_ref[g])
        scan_scr[g, 0:S5_SCAN_OFF, :] = jnp.zeros((S5_SCAN_OFF, LANES), F32)
        for lv in range(levels):
            sh = 1 << lv
            scan_scr[g, S5_SCAN_OFF:S5_SCAN_OFF + rows, :] = x
            prev = scan_scr[g, S5_SCAN_OFF - sh:S5_SCAN_OFF - sh + rows, :]
            prev = jnp.where(r_in >= sh, prev, 0.0)
            x = x + la_ref[lv, g] * prev + lb_ref[lv, g] * pltpu.roll(prev, B_STATE, 1)
        scan_scr[g, S5_SCAN_OFF:S5_SCAN_OFF + rows, :] = x
        x0 = scan_scr[g, S5_SCAN_OFF - 1:S5_SCAN_OFF - 1 + rows, :]
        x0 = jnp.where(r_in >= 1, x0, 0.0)
        ys.append(y + _dot(x0.astype(BF16), vt_ref[g]))
    for half in range(2):
        steps = block_transpose([yg[:, half * LANES:(half + 1) * LANES] for yg in ys])
        for sp in range(gps):
            out_ref[pl.ds(half * gps + sp, rows, stride=n), :] = steps[sp]


def _s5(y, u_col0, params, bsz, lp):
    m_t, w_t, v_t, lam_a, lam_b, d_flat = params
    r = y.shape[0]
    nch = lp // S5_CHUNK
    gps = S5_GROUPS_PER_STEP
    wide = S5_CHUNK * B_GROUP
    st = 2 * B_STATE
    levels = lam_a.shape[0]
    assert bsz % 2 == 0 and (1 << (levels - 1)) <= S5_SCAN_OFF and u_col0 % LANES == 0
    gspec = lambda *shape: pl.BlockSpec((gps,) + shape, lambda j, b: (j, 0, 0))
    lspec = pl.BlockSpec((levels, gps, 1, st), lambda j, b: (0, j, 0, 0))
    return pl.pallas_call(
        functools.partial(_s5_kernel, nch=nch),
        grid=(B_GROUPS // gps, bsz // 2),
        in_specs=[pl.BlockSpec((2 * lp, LANES), lambda j, b: (b, u_col0 // LANES + j)),
                  gspec(wide, wide), gspec(wide, st), gspec(st, wide), gspec(1, wide), lspec, lspec],
        out_specs=pl.BlockSpec((2 * lp, LANES), lambda j, b: (b, j)),
        out_shape=jax.ShapeDtypeStruct((r, B_CH), F32),
        scratch_shapes=[pltpu.VMEM((gps, S5_SCAN_OFF + 2 * nch, LANES), F32)],
        compiler_params=_cparams("parallel", "arbitrary"),
        name="s5",
    )(y, m_t, w_t, v_t, d_flat, lam_a, lam_b)


def _res_ln(h, mix, g, b, pad_rows):
    z = DN_ALPHA * h + mix
    mu = jnp.mean(z, axis=1, keepdims=True)
    zc = z - mu
    var = jnp.mean(zc * zc, axis=1, keepdims=True)
    out = zc * lax.rsqrt(var + LN_EPS) * g + b
    return jnp.where(pad_rows, 0.0, out)


def _pad_rows_mask(tm, lp):
    start = (pl.program_id(0) * tm) % lp
    pos = start + _iota((tm, 1), 0)
    pos = jnp.where(pos >= lp, pos - lp, pos)
    return pos < PAD


def _even_out_kernel(ha_ref, ys_ref, h_ref, wglu_ref, bglu_ref, wout_ref, g_ref, b_ref, out_ref,
                     *, lp):
    tm = h_ref.shape[0]
    yb = _gelu_tanh(ys_ref[...])
    hb = yb * _sigmoid(_dot(yb.astype(BF16), wglu_ref[...]) + bglu_ref[...])
    av = ha_ref.shape[1]
    mix = _dot(ha_ref[...].astype(BF16), wout_ref[:av, :]) + _dot(hb.astype(BF16), wout_ref[av:, :])
    out_ref[...] = _res_ln(h_ref[...], mix, g_ref[...], b_ref[...], _pad_rows_mask(tm, lp))


def _even_out(ha, ys, h, w_glu, b_glu, w_out, ln_g, ln_b, tm, lp):
    r, d = h.shape
    av, bc = ha.shape[1], ys.shape[1]
    full = lambda shape: pl.BlockSpec(shape, lambda i: (0, 0))
    return pl.pallas_call(
        functools.partial(_even_out_kernel, lp=lp),
        grid=(r // tm,),
        in_specs=[pl.BlockSpec((tm, av), lambda i: (i, 0)),
                  pl.BlockSpec((tm, bc), lambda i: (i, 0)),
                  pl.BlockSpec((tm, d), lambda i: (i, 0)),
                  full((bc, bc)), full((1, bc)), full((av + bc, d)), full((1, d)), full((1, d))],
        out_specs=pl.BlockSpec((tm, d), lambda i: (i, 0)),
        out_shape=jax.ShapeDtypeStruct((r, d), F32),
        compiler_params=_cparams("parallel"),
        name="even_out",
    )(ha, ys, h, w_glu, b_glu, w_out, ln_g, ln_b)


def _gdn_kernel(q_ref, k_ref, v_ref, z_ref, gcol_ref, grow_ref, conv_ref, pcol_ref, prow_ref,
                nw_ref, out_ref, s_scr, carry_scr, ext_scr):
    c = pl.program_id(1)

    @pl.when(c == 0)
    def _():
        s_scr[...] = jnp.zeros_like(s_scr)
        carry_scr[...] = jnp.zeros_like(carry_scr)

    n = CHUNK
    ri, ci = _tri_masks(n)
    incl = ri >= ci
    strict = ri > ci
    tri_l = jnp.where(incl, 1.0, 0.0).astype(BF16)
    tri_u = jnp.where(ri <= ci, 1.0, 0.0).astype(BF16)
    first = c == 0
    pad_c = jnp.logical_and(first, _iota((n, 1), 0) < PAD)
    pad_r = jnp.logical_and(first, _iota((1, n), 1) < PAD)

    gc = gcol_ref[...]
    beta_c = jnp.where(pad_c, 0.0, _sigmoid(gc))
    g_c = jnp.where(pad_c, 0.0, pcol_ref[1:2, :] * _softplus(gc + pcol_ref[0:1, :]))
    gcum_c = _dot01_left(tri_l, g_c)
    gr = grow_ref[0]
    g_r = jnp.where(pad_r, 0.0, prow_ref[1] * _softplus(gr + prow_ref[0]))
    gcum_r = _dot01_right(g_r, tri_u)

    width = q_ref.shape[1]
    ext_scr[0:8, :] = carry_scr[...]
    for j, ref in enumerate((q_ref, k_ref, v_ref)):
        ext_scr[8:8 + n, j * width:(j + 1) * width] = ref[...]
    carry_scr[...] = ext_scr[n:n + 8, :]

    def conv(col):
        acc = conv_ref[0:1, col] * ext_scr[5:5 + n, col]
        for j in range(1, C_CONV):
            acc = acc + conv_ref[j:j + 1, col] * ext_scr[5 + j:5 + j + n, col]
        return _silu(acc)

    def l2n(x, scale=1.0):
        return x * (lax.rsqrt(jnp.sum(x * x, axis=1, keepdims=True) + 1e-6) * scale)

    heads = range(C_HEADS)
    lanes = lambda base, h: slice(base + h * LANES, base + (h + 1) * LANES)
    q = [l2n(conv(lanes(0, h)), C_DK ** -0.5) for h in heads]
    k = [l2n(conv(lanes(width, h))) for h in heads]
    v = [conv(lanes(2 * width, h)) for h in heads]
    beta = [beta_c[:, h:h + 1] for h in heads]
    gi = [gcum_c[:, C_HEADS + h:C_HEADS + h + 1] for h in heads]
    gj = [gcum_r[C_HEADS + h:C_HEADS + h + 1, :] for h in heads]
    g_last = [g[n - 1:n, :] for g in gi]
    decay = [jnp.where(incl, jnp.exp(jnp.where(incl, gi[h] - gj[h], 0.0)), 0.0) for h in heads]
    eg = [jnp.exp(g) for g in gi]
    kb = [t.astype(BF16) for t in k]
    qkk = [_dot_nt(jnp.concatenate([q[h].astype(BF16), kb[h]], axis=0), kb[h]) for h in heads]
    attn = [(qkk[h][:n] * decay[h]).astype(BF16) for h in heads]
    ab = [jnp.where(strict, -(beta[h] * qkk[h][n:] * decay[h]), 0.0).astype(BF16) for h in heads]
    x = [jnp.concatenate([beta[h] * v[h], (beta[h] * eg[h]) * k[h]], axis=1) for h in heads]
    x = [x[h] + _dot(ab[h], x[h].astype(BF16)) for h in heads]
    for _ in range(5):
        ab = [_dot(t, t).astype(BF16) for t in ab]
        x = [x[h] + _dot(ab[h], x[h].astype(BF16)) for h in heads]
    s0 = [s_scr[h] for h in heads]
    q_dec = [(q[h] * eg[h]).astype(BF16) for h in heads]
    k_dec = [(k[h] * jnp.exp(g_last[h] - gi[h])).astype(BF16) for h in heads]
    ws_qs = [_dot(jnp.concatenate([x[h][:, LANES:].astype(BF16), q_dec[h]], axis=0), s0[h].astype(BF16))
             for h in heads]
    vb = [(x[h][:, :LANES] - ws_qs[h][:n]).astype(BF16) for h in heads]
    o = [ws_qs[h][n:] + _dot(attn[h], vb[h]) for h in heads]
    for h in heads:
        s_scr[h] = jnp.exp(g_last[h]) * s0[h] + _dot_tn(k_dec[h], vb[h])
    ms = [jnp.mean(t * t, axis=1, keepdims=True) for t in o]
    for h in heads:
        on = o[h] * lax.rsqrt(ms[h] + 1e-6) * nw_ref[...]
        out_ref[:, lanes(0, h)] = on * _silu(z_ref[:, lanes(0, h)])


def _gdn(y, gcol, grow3, conv_w, pcol, prow, norm_w, bsz, nc):
    r = y.shape[0]
    w = C_HEADS * C_DK
    row = lambda b, c: b * nc + c
    return pl.pallas_call(
        _gdn_kernel,
        grid=(bsz, nc),
        in_specs=[pl.BlockSpec((CHUNK, w), lambda b, c: (row(b, c), 0)),
                  pl.BlockSpec((CHUNK, w), lambda b, c: (row(b, c), 1)),
                  pl.BlockSpec((CHUNK, w), lambda b, c: (row(b, c), 2)),
                  pl.BlockSpec((CHUNK, w), lambda b, c: (row(b, c), 3)),
                  pl.BlockSpec((CHUNK, LANES), lambda b, c: (row(b, c), 0)),
                  pl.BlockSpec((1, 16, CHUNK), lambda b, c: (row(b, c), 0, 0)),
                  pl.BlockSpec((C_CONV, 3 * w), lambda b, c: (0, 0)),
                  pl.BlockSpec((2, LANES), lambda b, c: (0, 0)),
                  pl.BlockSpec((2, 16, CHUNK), lambda b, c: (0, 0, 0)),
                  pl.BlockSpec((1, LANES), lambda b, c: (0, 0))],
        out_specs=pl.BlockSpec((CHUNK, w), lambda b, c: (row(b, c), 0)),
        out_shape=jax.ShapeDtypeStruct((r, w), F32),
        scratch_shapes=[pltpu.VMEM((C_HEADS, C_DK, C_DK), F32),
                        pltpu.VMEM((8, 3 * w), F32),
                        pltpu.VMEM((CHUNK + 8, 3 * w), F32)],
        compiler_params=_cparams("parallel", "arbitrary"),
        name="gdn",
    )(y, y, y, y, gcol, grow3, conv_w, pcol, prow, norm_w)


def _odd_out_kernel(o_ref, h_ref, wout_ref, g_ref, b_ref, out_ref, *, lp):
    tm = h_ref.shape[0]
    mix = _dot(o_ref[...].astype(BF16), wout_ref[...])
    out_ref[...] = _res_ln(h_ref[...], mix, g_ref[...], b_ref[...], _pad_rows_mask(tm, lp))


def _odd_out(o, h, w_out, ln_g, ln_b, tm, lp):
    r, d = h.shape
    full = lambda shape: pl.BlockSpec(shape, lambda i: (0, 0))
    return pl.pallas_call(
        functools.partial(_odd_out_kernel, lp=lp),
        grid=(r // tm,),
        in_specs=[pl.BlockSpec((tm, d), lambda i: (i, 0)),
                  pl.BlockSpec((tm, d), lambda i: (i, 0)),
                  full((d, d)), full((1, d)), full((1, d))],
        out_specs=pl.BlockSpec((tm, d), lambda i: (i, 0)),
        out_shape=jax.ShapeDtypeStruct((r, d), F32),
        compiler_params=_cparams("parallel"),
        name="odd_out",
    )(o, h, w_out, ln_g, ln_b)


def _router_kernel(x_ref, w_ref, b_ref, meta_ref, gate_ref, cnt_ref, base_scr):
    @pl.when(pl.program_id(0) == 0)
    def _():
        base_scr[...] = jnp.zeros_like(base_scr)

    tm = x_ref.shape[0]
    logits = _dot(x_ref[...].astype(BF16), w_ref[...]) + b_ref[...]
    lane = _iota((tm, LANES), 1)
    lane_f = lane.astype(F32)
    work = logits
    vals, sels = [], []
    onehot = jnp.zeros((tm, LANES), F32)
    for _ in range(TOP_K):
        m = jnp.max(work, axis=1, keepdims=True)
        idx = jnp.min(jnp.where(work == m, lane_f, float(LANES)), axis=1, keepdims=True)
        sel = lane_f == idx
        vals.append(m)
        sels.append((sel, idx))
        onehot = onehot + jnp.where(sel, 1.0, 0.0)
        work = jnp.where(sel, -jnp.inf, work)
    ri, ci = _tri_masks(tm)
    tri = jnp.where(ri > ci, 1.0, 0.0).astype(BF16)
    before = _dot(tri, onehot.astype(BF16)) + base_scr[0:1, :]
    base_scr[...] = base_scr[...] + jnp.sum(onehot, axis=0, keepdims=True)
    cnt_ref[...] = base_scr[...]
    es = [jnp.exp(v - vals[0]) for v in vals]
    tot = es[0] + es[1] + es[2] + es[3]
    meta = jnp.zeros((tm, LANES), F32)
    gate = jnp.zeros((tm, LANES), F32)
    for k, (sel, idx) in enumerate(sels):
        rank = jnp.sum(jnp.where(sel, before, 0.0), axis=1, keepdims=True)
        meta = jnp.where(lane == k, idx, meta)
        meta = jnp.where(lane == TOP_K + k, rank, meta)
        gate = jnp.where(lane == k, es[k] / tot, gate)
    gate_ref[...] = gate
    pick = jnp.where(_iota((8, LANES), 0) == _iota((8, LANES), 1), 1.0, 0.0).astype(BF16)
    hi, mid, lo = _split3(meta)
    meta_ref[...] = (_dot_nt(pick, hi) + _dot_nt(pick, mid) + _dot_nt(pick, lo)).astype(jnp.int32)


def _router(h, w_r, b_r, tm):
    r, d = h.shape
    return pl.pallas_call(
        _router_kernel,
        grid=(r // tm,),
        in_specs=[pl.BlockSpec((tm, d), lambda i: (i, 0)),
                  pl.BlockSpec((d, LANES), lambda i: (0, 0)),
                  pl.BlockSpec((1, LANES), lambda i: (0, 0))],
        out_specs=[pl.BlockSpec((8, tm), lambda i: (0, i)),
                   pl.BlockSpec((tm, LANES), lambda i: (i, 0)),
                   pl.BlockSpec((8, LANES), lambda i: (0, 0))],
        out_shape=[jax.ShapeDtypeStruct((8, r), jnp.int32),
                   jax.ShapeDtypeStruct((r, LANES), F32),
                   jax.ShapeDtypeStruct((8, LANES), F32)],
        scratch_shapes=[pltpu.VMEM((8, LANES), F32)],
        compiler_params=_cparams("arbitrary"),
        name="router",
    )(h, w_r, b_r)


ROW_DMA_UNROLL = 8
ROW_DMA_WAIT_GROUP = 64


def _issue_row_copies(make_copy, tm):
    def issue(i, carry):
        for u in range(ROW_DMA_UNROLL):
            for k in range(TOP_K):
                make_copy(i * ROW_DMA_UNROLL + u, k).start(priority=(u * TOP_K + k) % 2)
        return carry

    lax.fori_loop(0, tm // ROW_DMA_UNROLL, issue, 0)


def _wait_row_copies(make_copy, tm):
    def drain(i, carry):
        for _ in range(ROW_DMA_WAIT_GROUP):
            make_copy(0, 0).wait()
        return carry

    lax.fori_loop(0, tm * TOP_K // ROW_DMA_WAIT_GROUP, drain, 0)


def _dispatch_kernel(dest_ref, x_hbm, xs_in, xs_hbm, stage, sem_in, sem_out, *, tm):
    del xs_in
    i = pl.program_id(0)
    n = pl.num_programs(0)
    cur = i % 3

    def load(tile, into):
        return pltpu.make_async_copy(x_hbm.at[pl.ds(tile * tm, tm)], stage.at[into], sem_in.at[into])

    def scatter(src_slot, sem_slot):
        def make_copy(t, k):
            return pltpu.make_async_copy(stage.at[src_slot, pl.ds(t, 1)],
                                         xs_hbm.at[pl.ds(dest_ref[k * tm + t], 1)], sem_out.at[sem_slot])
        return make_copy

    @pl.when(i == 0)
    def _():
        load(0, 0).start()

    @pl.when(i + 1 < n)
    def _():
        load(i + 1, (i + 1) % 3).start()

    load(i, cur).wait()
    _issue_row_copies(scatter(cur, i % 2), tm)

    @pl.when(i >= 1)
    def _():
        _wait_row_copies(scatter((i + 2) % 3, (i + 1) % 2), tm)

    @pl.when(i == n - 1)
    def _():
        _wait_row_copies(scatter(cur, i % 2), tm)


def _dispatch(h, dest_flat, xs0, tm):
    r, d = h.shape
    n_slots = xs0.shape[0]
    return pl.pallas_call(
        functools.partial(_dispatch_kernel, tm=tm),
        grid=(r // tm,),
        in_specs=[pl.BlockSpec((tm * TOP_K,), lambda i: (i,), memory_space=pltpu.SMEM),
                  pl.BlockSpec(memory_space=pl.ANY),
                  pl.BlockSpec(memory_space=pl.ANY)],
        out_specs=pl.BlockSpec(memory_space=pl.ANY),
        out_shape=jax.ShapeDtypeStruct((n_slots, d), h.dtype),
        scratch_shapes=[pltpu.VMEM((3, tm, d), h.dtype), pltpu.SemaphoreType.DMA((3,)),
                        pltpu.SemaphoreType.DMA((2,))],
        input_output_aliases={2: 0},
        compiler_params=_cparams("arbitrary"),
        name="moe_dispatch",
    )(dest_flat, h, xs0)


EXPERT_BLOCKS_PER_STEP = 2


def _expert_kernel(be_ref, nu_ref, slot_ref, nxt_ref, x_ref, bgu_ref, bd_ref, wgu_hbm, wd_hbm, out_ref,
                   wgu_buf, wd_buf, wgu_scr, wd_scr, sems, *, layer):
    tile = 2 * LANES
    n_tiles = wgu_buf.shape[2] // tile

    def weight_copies(expert, into):
        return (pltpu.make_async_copy(wgu_hbm.at[layer, expert], wgu_buf.at[into], sems.at[0, into]),
                pltpu.make_async_copy(wd_hbm.at[layer, expert], wd_buf.at[into], sems.at[1, into]))

    def one_block(b, rows):
        used = b < nu_ref[0]
        e = be_ref[b]
        fresh = jnp.logical_and(used, jnp.logical_or(b == 0, e != be_ref[jnp.maximum(b - 1, 0)]))
        slot = slot_ref[b]

        @pl.when(jnp.logical_and(used, b == 0))
        def _():
            for cp in weight_copies(e, slot):
                cp.start()

        @pl.when(fresh)
        def _():
            for cp in weight_copies(e, slot):
                cp.wait()

            @pl.when(nxt_ref[b] >= 0)
            def _():
                for cp in weight_copies(nxt_ref[b], 1 - slot):
                    cp.start()

            r = _iota((tile, tile), 0)
            c = _iota((tile, tile), 1)
            src = jnp.where(c < LANES, 2 * c, 2 * (c - LANES) + 1)
            perm = jnp.where(r == src, 1.0, 0.0).astype(BF16)
            for j in range(n_tiles):
                cols = slice(j * tile, (j + 1) * tile)
                wgu_scr[:, cols] = _dot(wgu_buf[slot, :, cols].astype(BF16), perm).astype(BF16)
            wd_scr[...] = wd_buf[slot].astype(BF16)

        @pl.when(used)
        def _():
            xb = x_ref[rows, :].astype(BF16)
            h = _dot(xb, wgu_scr[...]) + bgu_ref[e]
            acts = []
            for j in range(n_tiles):
                gate = jnp.minimum(h[:, j * tile:j * tile + LANES], SWIGLU_LIMIT)
                up = jnp.clip(h[:, j * tile + LANES:(j + 1) * tile], -SWIGLU_LIMIT, SWIGLU_LIMIT)
                acts.append(((up + 1.0) * gate * _sigmoid(SWIGLU_ALPHA * gate)).astype(BF16))
            act = jnp.concatenate(acts, axis=1)
            out_ref[rows, :] = _dot(act, wd_scr[...]) + bd_ref[e]

        @pl.when(jnp.logical_not(used))
        def _():
            out_ref[rows, :] = jnp.zeros((MOE_BLOCK, out_ref.shape[1]), out_ref.dtype)

    for s in range(EXPERT_BLOCKS_PER_STEP):
        one_block(pl.program_id(0) * EXPERT_BLOCKS_PER_STEP + s, slice(s * MOE_BLOCK, (s + 1) * MOE_BLOCK))


def _experts(xs, block_e, n_used, slot_of, next_e, w_gu_all, b_gu, w_d_all, b_d, layer):
    n_slots, d = xs.shape
    n_blocks = n_slots // MOE_BLOCK
    de2 = w_gu_all.shape[3]
    de = w_d_all.shape[2]
    rows = EXPERT_BLOCKS_PER_STEP * MOE_BLOCK
    assert n_blocks % EXPERT_BLOCKS_PER_STEP == 0
    grid_spec = pltpu.PrefetchScalarGridSpec(
        num_scalar_prefetch=4,
        grid=(n_blocks // EXPERT_BLOCKS_PER_STEP,),
        in_specs=[pl.BlockSpec((rows, d), lambda i, be, nu, sl, nx: (i, 0)),
                  pl.BlockSpec((N_EXPERTS, 1, de2), lambda i, be, nu, sl, nx: (0, 0, 0)),
                  pl.BlockSpec((N_EXPERTS, 1, d), lambda i, be, nu, sl, nx: (0, 0, 0)),
                  pl.BlockSpec(memory_space=pl.ANY),
                  pl.BlockSpec(memory_space=pl.ANY)],
        out_specs=pl.BlockSpec((rows, d), lambda i, be, nu, sl, nx: (i, 0)),
        scratch_shapes=[pltpu.VMEM((2, d, de2), F32), pltpu.VMEM((2, de, d), F32),
                        pltpu.VMEM((d, de2), BF16), pltpu.VMEM((de, d), BF16),
                        pltpu.SemaphoreType.DMA((2, 2))],
    )
    return pl.pallas_call(
        functools.partial(_expert_kernel, layer=layer),
        grid_spec=grid_spec,
        out_shape=jax.ShapeDtypeStruct((n_slots, d), F32),
        compiler_params=_cparams("arbitrary"),
        name="moe_experts",
    )(block_e, n_used, slot_of, next_e, xs, b_gu, b_d, w_gu_all, w_d_all)


def _combine_kernel(dest_ref, dest_next_ref, gate_ref, h_ref, g_ref, b_ref, eo_hbm, out_ref, buf, sems,
                    *, lp):
    tm = h_ref.shape[0]
    i = pl.program_id(0)
    slot = i % 2

    def gather(idx_ref, into):
        def make_copy(t, k):
            return pltpu.make_async_copy(eo_hbm.at[pl.ds(idx_ref[k * tm + t], 1)],
                                         buf.at[into, k, pl.ds(t, 1)], sems.at[into])
        return make_copy

    @pl.when(i == 0)
    def _():
        _issue_row_copies(gather(dest_ref, slot), tm)

    @pl.when(i + 1 < pl.num_programs(0))
    def _():
        _issue_row_copies(gather(dest_next_ref, 1 - slot), tm)

    _wait_row_copies(gather(dest_ref, slot), tm)

    gates = gate_ref[...]
    ffn = gates[:, 0:1] * buf[slot, 0]
    for k in range(1, TOP_K):
        ffn = ffn + gates[:, k:k + 1] * buf[slot, k]
    out_ref[...] = _res_ln(h_ref[...], ffn, g_ref[...], b_ref[...], _pad_rows_mask(tm, lp))


def _combine(eo, dest_flat, gates, h, ln_g, ln_b, tm, lp):
    r, d = h.shape
    n_tiles = r // tm
    full = lambda shape: pl.BlockSpec(shape, lambda i: (0, 0))
    return pl.pallas_call(
        functools.partial(_combine_kernel, lp=lp),
        grid=(n_tiles,),
        in_specs=[pl.BlockSpec((tm * TOP_K,), lambda i: (i,), memory_space=pltpu.SMEM),
                  pl.BlockSpec((tm * TOP_K,), lambda i: (jnp.minimum(i + 1, n_tiles - 1),),
                               memory_space=pltpu.SMEM),
                  pl.BlockSpec((tm, LANES), lambda i: (i, 0)),
                  pl.BlockSpec((tm, d), lambda i: (i, 0)),
                  full((1, d)), full((1, d)),
                  pl.BlockSpec(memory_space=pl.ANY)],
        out_specs=pl.BlockSpec((tm, d), lambda i: (i, 0)),
        out_shape=jax.ShapeDtypeStruct((r, d), F32),
        scratch_shapes=[pltpu.VMEM((2, TOP_K, tm, d), F32), pltpu.SemaphoreType.DMA((2,))],
        compiler_params=_cparams("arbitrary"),
        name="moe_combine",
    )(dest_flat, dest_flat, gates, h, ln_g, ln_b, eo)


def _moe(h, w_router, b_router, w_gu_all, b_gu, w_d_all, b_d, ln_g, ln_b, tm, lp, xs_prev, layer):
    r, d = h.shape
    w_r = jnp.pad(w_router, ((0, 0), (0, LANES - N_EXPERTS))).astype(BF16)
    b_r = jnp.pad(b_router, (0, LANES - N_EXPERTS), constant_values=NEG)[None, :]
    meta, gates, cnt = _router(h, w_r, b_r, 2 * tm if r % (2 * tm) == 0 else tm)
    counts = cnt[0, :N_EXPERTS].astype(jnp.int32)
    padded = (counts + MOE_BLOCK - 1) // MOE_BLOCK * MOE_BLOCK
    ends_p = jnp.cumsum(padded)
    pstart = ends_p - padded
    n_blocks = -(-(r * TOP_K) // MOE_BLOCK) + N_EXPERTS
    n_slots = n_blocks * MOE_BLOCK
    block_start = jnp.arange(n_blocks, dtype=jnp.int32) * MOE_BLOCK
    block_e = jnp.minimum(jnp.sum((ends_p[None, :] <= block_start[:, None]).astype(jnp.int32), axis=1),
                          N_EXPERTS - 1)
    n_used = (ends_p[-1:] // MOE_BLOCK).astype(jnp.int32)
    dest = pstart[meta[:TOP_K]] + meta[TOP_K:2 * TOP_K]
    dest = dest.reshape(TOP_K, r // tm, tm).transpose(1, 0, 2).reshape(-1)
    nonempty = padded > 0
    ids = jnp.arange(N_EXPERTS, dtype=jnp.int32)
    seg = jnp.cumsum(nonempty.astype(jnp.int32)) - nonempty.astype(jnp.int32)
    later = jnp.min(jnp.where(jnp.logical_and(nonempty[None, :], ids[None, :] > ids[:, None]),
                              ids[None, :], N_EXPERTS), axis=1)
    slot_of = (seg % 2)[block_e]
    next_e = jnp.where(later == N_EXPERTS, -1, later)[block_e]
    xs = _dispatch(h, dest, jnp.zeros((n_slots, d), h.dtype) if xs_prev is None else xs_prev, tm)
    half = w_gu_all.shape[3] // 2
    b_gu_t = b_gu.reshape(N_EXPERTS, half // LANES, LANES, 2).transpose(0, 1, 3, 2)
    b_gu_t = b_gu_t.reshape(N_EXPERTS, 1, 2 * half)
    eo = _experts(xs, block_e, n_used, slot_of, next_e, w_gu_all, b_gu_t, w_d_all, b_d[:, None, :], layer)
    return _combine(eo, dest, gates, h, ln_g, ln_b, tm, lp), xs


def _pad_heads(w, heads, dim):
    k = w.shape[0]
    return jnp.pad(w.reshape(k, heads, dim), ((0, 0), (0, 0), (0, LANES - dim))).reshape(k, heads * LANES)


def _gate_weights(wg):
    k, n = wg.shape
    return (jnp.pad(wg, ((0, 0), (0, LANES - n))).astype(BF16),
            jnp.pad(wg.T, ((0, 16 - n), (0, 0))).astype(BF16))


def _even_layer(h, p, bsz, nc, tm, lp):
    r = h.shape[0]
    w_in = p['w_in']
    aq, av = A_HEADS * A_DQK, A_HEADS * A_DV
    o0 = 2 * aq + 2 * av
    w_main = jnp.concatenate([_pad_heads(w_in[:, :aq], A_HEADS, A_DQK),
                              _pad_heads(w_in[:, aq:2 * aq], A_HEADS, A_DQK),
                              w_in[:, 2 * aq:2 * aq + av], w_in[:, 2 * aq + av:o0],
                              w_in[:, o0 + 2 * A_HEADS:]], axis=1).astype(BF16)
    w_gate, w_gate_t = _gate_weights(w_in[:, o0:o0 + 2 * A_HEADS])
    y, gcol, grow = _in_proj(h, w_main, w_gate, w_gate_t, _chunk_tile(r // bsz, 704), w_main.shape[1])
    gb = p['gate_bias']
    bias_col = jnp.pad(gb, (0, LANES - gb.shape[0]))[None, :]
    bias_row = jnp.broadcast_to(jnp.pad(gb, (0, 16 - gb.shape[0]))[:, None], (16, CHUNK))
    ha = _mlstm(y, gcol, grow, bias_col, bias_row, p['head_norm'][None, :], bsz, nc)
    levels = max(1, math.ceil(math.log2(lp // S5_CHUNK)))
    ys = _s5(y, 4 * A_HEADS * LANES,
             _s5_params(p['a_re'], p['a_im'], p['log_step'], p['b_re'], p['b_im'],
                        p['c_re'], p['c_im'], p['d'], levels), bsz, lp)
    return _even_out(ha, ys, h, p['w_glu'].astype(BF16), p['b_glu'][None, :], p['w_out'].astype(BF16),
                     p['ln_g'][None, :], p['ln_b'][None, :], tm, lp)


def _odd_layer(h, p, bsz, nc, tm, lp):
    w_in = p['w_in']
    cw = C_HEADS * C_DK
    w_main = w_in[:, :4 * cw].astype(BF16)
    w_gate, w_gate_t = _gate_weights(w_in[:, 4 * cw:])
    y, gcol, grow = _in_proj(h, w_main, w_gate, w_gate_t, _chunk_tile(h.shape[0] // bsz, 704),
                             w_main.shape[1])
    neg_a = -jnp.exp(p['a_log'])
    zeros = jnp.zeros((C_HEADS,), F32)
    dt16 = jnp.concatenate([zeros, p['dt_bias']])
    na16 = jnp.concatenate([zeros, neg_a])
    pcol = jnp.pad(jnp.stack([dt16, na16]), ((0, 0), (0, LANES - 16)))
    prow = jnp.broadcast_to(jnp.stack([dt16, na16])[:, :, None], (2, 16, CHUNK))
    o = _gdn(y, gcol, grow, p['conv'], pcol, prow, p['norm'][None, :], bsz, nc)
    return _odd_out(o, h, p['w_out'].astype(BF16), p['ln_g'][None, :], p['ln_b'][None, :], tm, lp)


def kernel(x, meta_tokens, ln_g, ln_b, ev_w_in, ev_gate_bias, ev_head_norm, s5_a_re, s5_a_im, s5_log_step, s5_b_re, s5_b_im, s5_c_re, s5_c_im, s5_d, s5_w_glu, s5_b_glu, ev_w_out, od_w_in, od_conv, od_a_log, od_dt_bias, od_norm, od_w_out, moe_w_router, moe_b_router, moe_w_gate_up, moe_b_gate_up, moe_w_down, moe_b_down):
    bsz, seq, d = x.shape
    lp = PAD + N_META + seq
    assert lp % CHUNK == 0 and d == D_MODEL
    nc = lp // CHUNK
    r = bsz * lp
    assert r % MOE_BLOCK == 0 and lp >= MOE_BLOCK
    tm = 512 if (r % 512 == 0 and lp >= 512) else MOE_BLOCK
    meta = jnp.broadcast_to(meta_tokens[None], (bsz, N_META, d)).astype(x.dtype)
    h = jnp.concatenate([jnp.zeros((bsz, PAD, d), x.dtype), meta, x], axis=1).reshape(bsz * lp, d)
    xs = None
    for layer in range(ln_g.shape[0]):
        j = layer // 2
        if layer % 2 == 0:
            p = dict(w_in=ev_w_in[j], gate_bias=ev_gate_bias[j], head_norm=ev_head_norm[j],
                     a_re=s5_a_re[j], a_im=s5_a_im[j], log_step=s5_log_step[j], b_re=s5_b_re[j],
                     b_im=s5_b_im[j], c_re=s5_c_re[j], c_im=s5_c_im[j], d=s5_d[j],
                     w_glu=s5_w_glu[j], b_glu=s5_b_glu[j], w_out=ev_w_out[j],
                     ln_g=ln_g[layer, 0], ln_b=ln_b[layer, 0])
            h = _even_layer(h, p, bsz, nc, tm, lp)
        else:
            p = dict(w_in=od_w_in[j], conv=od_conv[j], a_log=od_a_log[j], dt_bias=od_dt_bias[j],
                     norm=od_norm[j], w_out=od_w_out[j], ln_g=ln_g[layer, 0], ln_b=ln_b[layer, 0])
            h = _odd_layer(h, p, bsz, nc, tm, lp)
        h, xs = _moe(h, moe_w_router[layer], moe_b_router[layer], moe_w_gate_up,
                     moe_b_gate_up[layer], moe_w_down, moe_b_down[layer],
                     ln_g[layer, 1][None, :], ln_b[layer, 1][None, :], MOE_BLOCK, lp, xs, layer)
    return h.reshape(bsz, lp, d)[:, PAD + N_META:]
```

```python
import functools
import math

import jax
import jax.numpy as jnp
from jax import lax
from jax.experimental import pallas as pl
from jax.experimental.pallas import tpu as pltpu

F32 = jnp.float32
BF16 = jnp.bfloat16

D_MODEL = 1024
DEPTH = 4
N_META = 16
CHUNK = 64
PAD = CHUNK - N_META
NEG = -1e30
LN_EPS = 1e-5
DN_ALPHA = (2.0 * DEPTH) ** 0.25

A_HEADS = 4
A_DQK = D_MODEL // 16
A_DV = D_MODEL // 8
A_GATE_CAP = 15.0
B_CH = D_MODEL // 2
B_GROUP = 16
B_GROUPS = B_CH // B_GROUP
B_STATE = 64
S5_CHUNK = 16
C_HEADS = D_MODEL // 128
C_DK = 128
C_CONV = 4
N_EXPERTS = 32
TOP_K = 4
SWIGLU_LIMIT = 7.0
SWIGLU_ALPHA = 1.702
MOE_BLOCK = 256

LANES = 128
VMEM_LIMIT = 56 * 1024 * 1024


def _cparams(*sem):
    return pltpu.CompilerParams(dimension_semantics=sem, vmem_limit_bytes=VMEM_LIMIT)


def _chunk_tile(lp, target):
    best = CHUNK
    for t in range(CHUNK, target + 1, CHUNK):
        if lp % t == 0:
            best = t
    return best


def _dot(a, b):
    return jnp.dot(a, b, preferred_element_type=F32)


def _dot_nt(a, b):
    return lax.dot_general(a, b, (((1,), (1,)), ((), ())), preferred_element_type=F32)


def _dot_tn(a, b):
    return lax.dot_general(a, b, (((0,), (0,)), ((), ())), preferred_element_type=F32)


def _split3(x):
    hi = x.astype(BF16)
    r1 = x - hi.astype(F32)
    mid = r1.astype(BF16)
    lo = (r1 - mid.astype(F32)).astype(BF16)
    return hi, mid, lo


def _dot01_left(t01, x):
    hi, mid, lo = _split3(x)
    return _dot(t01, hi) + _dot(t01, mid) + _dot(t01, lo)


def _dot01_right(x, t01):
    hi, mid, lo = _split3(x)
    return _dot(hi, t01) + _dot(mid, t01) + _dot(lo, t01)


def _row_sum_lanes(x):
    ones = jnp.ones((x.shape[1], LANES), BF16)
    hi = x.astype(BF16)
    lo = (x - hi.astype(F32)).astype(BF16)
    return _dot(hi, ones) + _dot(lo, ones)


def _sigmoid(x):
    return 1.0 / (1.0 + jnp.exp(-x))


def _softplus(x):
    return jnp.maximum(x, 0.0) + jnp.log(1.0 + jnp.exp(-jnp.abs(x)))


def _log_sigmoid(x):
    return -_softplus(-x)


def _silu(x):
    return x * _sigmoid(x)


def _gelu_tanh(x):
    c = math.sqrt(2.0 / math.pi)
    return 0.5 * x * (1.0 + jnp.tanh(c * (x + 0.044715 * (x * x * x))))


def _iota(shape, dim):
    return lax.broadcasted_iota(jnp.int32, shape, dim)


def _tri_masks(n):
    r = _iota((n, n), 0)
    c = _iota((n, n), 1)
    return r, c


def _proj_kernel(x_ref, w_ref, wg_ref, wgt_ref, y_ref, gcol_ref, grow_ref):
    xb = x_ref[...].astype(BF16)
    y_ref[...] = _dot(xb, w_ref[...])

    @pl.when(pl.program_id(1) == 0)
    def _():
        gcol_ref[...] = _dot(xb, wg_ref[...])
        grow = _dot_nt(wgt_ref[...], xb)
        for j in range(grow_ref.shape[0]):
            grow_ref[j] = grow[:, j * CHUNK:(j + 1) * CHUNK]


def _in_proj(h, w_main, w_gate, w_gate_t, tm, tn):
    r, d = h.shape
    n = w_main.shape[1]
    cpt = tm // CHUNK
    return pl.pallas_call(
        _proj_kernel,
        grid=(r // tm, n // tn),
        in_specs=[pl.BlockSpec((tm, d), lambda i, j: (i, 0)),
                  pl.BlockSpec((d, tn), lambda i, j: (0, j)),
                  pl.BlockSpec((d, LANES), lambda i, j: (0, 0)),
                  pl.BlockSpec((16, d), lambda i, j: (0, 0))],
        out_specs=[pl.BlockSpec((tm, tn), lambda i, j: (i, j)),
                   pl.BlockSpec((tm, LANES), lambda i, j: (i, 0)),
                   pl.BlockSpec((cpt, 16, CHUNK), lambda i, j: (i, 0, 0))],
        out_shape=[jax.ShapeDtypeStruct((r, n), F32),
                   jax.ShapeDtypeStruct((r, LANES), F32),
                   jax.ShapeDtypeStruct((r // CHUNK, 16, CHUNK), F32)],
        compiler_params=_cparams("parallel", "arbitrary"),
        name="in_proj",
    )(h, w_main, w_gate, w_gate_t)


def _mlstm_kernel(q_ref, k_ref, v_ref, o_ref, gcol_ref, grow_ref, bcol_ref, brow_ref, hn_ref,
                  out_ref, c_scr, m_scr):
    c = pl.program_id(1)

    @pl.when(c == 0)
    def _():
        c_scr[...] = jnp.zeros_like(c_scr)
        m_scr[...] = jnp.zeros_like(m_scr)

    n = CHUNK
    ri, ci = _tri_masks(n)
    causal = ri >= ci
    tri_l = jnp.where(causal, 1.0, 0.0).astype(BF16)
    tri_u = jnp.where(ri <= ci, 1.0, 0.0).astype(BF16)
    first = c == 0
    pad_c = jnp.logical_and(first, _iota((n, 1), 0) < PAD)
    pad_r = jnp.logical_and(first, _iota((1, n), 1) < PAD)

    i_c, b_c, i_r, b_r = [], [], [], []
    for sq in range(q_ref.shape[0]):
        gc = A_GATE_CAP * jnp.tanh((gcol_ref[sq] + bcol_ref[...]) * (1.0 / A_GATE_CAP))
        i_c.append(jnp.where(pad_c, NEG, gc))
        b_c.append(_dot01_left(tri_l, jnp.where(pad_c, 0.0, _log_sigmoid(gc))))
        gr = A_GATE_CAP * jnp.tanh((grow_ref[sq] + brow_ref[...]) * (1.0 / A_GATE_CAP))
        i_r.append(jnp.where(pad_r, NEG, gr))
        b_r.append(_dot01_right(jnp.where(pad_r, 0.0, _log_sigmoid(gr)), tri_u))

    one_col = jnp.where(_iota((n, A_DV), 1) == 0, 1.0, 0.0).astype(BF16)
    chains = [(sq, hd) for sq in range(q_ref.shape[0]) for hd in range(A_HEADS)]
    heads = range(len(chains))
    sl = [slice(hd * LANES, (hd + 1) * LANES) for _, hd in chains]
    q = [q_ref[sq, :, sl[h]].astype(BF16) for h, (sq, _) in enumerate(chains)]
    kf = [k_ref[sq, :, sl[h]] * (A_DQK ** -0.5) for h, (sq, _) in enumerate(chains)]
    vext = [jnp.concatenate([v_ref[sq, :, sl[h]].astype(BF16), one_col], axis=1)
            for h, (sq, _) in enumerate(chains)]
    bi = [b_c[sq][:, A_HEADS + hd:A_HEADS + hd + 1] for sq, hd in chains]
    ii = [i_c[sq][:, hd:hd + 1] for sq, hd in chains]
    bj = [b_r[sq][A_HEADS + hd:A_HEADS + hd + 1, :] for sq, hd in chains]
    ij = [i_r[sq][hd:hd + 1, :] for sq, hd in chains]
    b_last = [t[n - 1:n, :] for t in bi]
    a_end = [b_last[h] - bi[h] + ii[h] for h in heads]
    m_loc = [jnp.max(t, axis=0, keepdims=True) for t in a_end]
    m0 = [m_scr[h][0:1, 0:1] for h in heads]
    d_intra = [jnp.where(causal, bi[h] - bj[h] + ij[h], NEG) for h in heads]
    d_inter = [bi[h] + m0[h] for h in heads]
    m_row = [jnp.maximum(jnp.max(d_intra[h], axis=1, keepdims=True), d_inter[h]) for h in heads]
    s = [jnp.exp(d_inter[h] - m_row[h]) for h in heads]
    qk = [(_dot_nt(q[h], kf[h].astype(BF16)) * jnp.exp(d_intra[h] - m_row[h])).astype(BF16) for h in heads]
    cext = [c_scr[h] for h in heads]
    num_ext = [_dot(qk[h], vext[h]) + s[h] * _dot(q[h], cext[h].astype(BF16)) for h in heads]
    w_end = [jnp.exp(a_end[h] - m_loc[h]) for h in heads]
    d_ext = [_dot_tn((kf[h] * w_end[h]).astype(BF16), vext[h]) for h in heads]
    for h in heads:
        m_new = jnp.maximum(b_last[h] + m0[h], m_loc[h])
        c_scr[h] = jnp.exp(b_last[h] + m0[h] - m_new) * cext[h] + jnp.exp(m_loc[h] - m_new) * d_ext[h]
        m_scr[h] = jnp.broadcast_to(m_new, (8, LANES))
    hh = [num_ext[h][:, :A_DV] / jnp.maximum(jnp.abs(num_ext[h][:, A_DV:A_DV + 1]), jnp.exp(-m_row[h]))
          for h in heads]
    mu = [_row_sum_lanes(t) * (1.0 / A_DV) for t in hh]
    hc = [hh[h] - mu[h] for h in heads]
    var = [_row_sum_lanes(t * t) * (1.0 / A_DV) for t in hc]
    for h, (sq, _) in enumerate(chains):
        hnorm = hc[h] * lax.rsqrt(var[h] + 1e-6) * hn_ref[:, sl[h]]
        out_ref[sq, :, sl[h]] = _sigmoid(o_ref[sq, :, sl[h]]) * hnorm


MLSTM_SEQS_PER_STEP = 2


def _mlstm(y, gcol, grow3, bias_col, bias_row, head_norm, bsz, nc):
    r = y.shape[0]
    lp = r // bsz
    w = A_HEADS * LANES
    ns = MLSTM_SEQS_PER_STEP if bsz % MLSTM_SEQS_PER_STEP == 0 else 1
    y4 = y.reshape(bsz // ns, ns, lp, y.shape[1])
    col = lambda j: pl.BlockSpec((None, ns, CHUNK, w), lambda b, c: (b, 0, c, j))
    out = pl.pallas_call(
        _mlstm_kernel,
        grid=(bsz // ns, nc),
        in_specs=[col(0), col(1), col(2), col(3),
                  pl.BlockSpec((None, ns, CHUNK, LANES), lambda b, c: (b, 0, c, 0)),
                  pl.BlockSpec((None, ns, None, 16, CHUNK), lambda b, c: (b, 0, c, 0, 0)),
                  pl.BlockSpec((1, LANES), lambda b, c: (0, 0)),
                  pl.BlockSpec((16, CHUNK), lambda b, c: (0, 0)),
                  pl.BlockSpec((1, w), lambda b, c: (0, 0))],
        out_specs=pl.BlockSpec((None, ns, CHUNK, w), lambda b, c: (b, 0, c, 0)),
        out_shape=jax.ShapeDtypeStruct((bsz // ns, ns, lp, w), F32),
        scratch_shapes=[pltpu.VMEM((ns * A_HEADS, LANES, 2 * LANES), F32),
                        pltpu.VMEM((ns * A_HEADS, 8, LANES), F32)],
        compiler_params=_cparams("parallel", "arbitrary"),
        name="mlstm",
    )(y4, y4, y4, y4, gcol.reshape(bsz // ns, ns, lp, LANES), grow3.reshape(bsz // ns, ns, nc, 16, CHUNK),
      bias_col, bias_row, head_norm)
    return out.reshape(r, w)


def _s5_params(a_re, a_im, log_step, b_re, b_im, c_re, c_im, d_skip, levels):
    hp = lax.Precision.HIGHEST
    g, p = a_re.shape
    dt = jnp.exp(log_step)[:, None]
    mag = jnp.exp(a_re * dt)
    lb_re, lb_im = mag * jnp.cos(a_im * dt), mag * jnp.sin(a_im * dt)
    inv = 1.0 / (a_re * a_re + a_im * a_im)
    zr, zi = lb_re - 1.0, lb_im
    fr = (zr * a_re + zi * a_im) * inv
    fi = (zi * a_re - zr * a_im) * inv
    bb_re = fr[..., None] * b_re - fi[..., None] * b_im
    bb_im = fr[..., None] * b_im + fi[..., None] * b_re
    n = S5_CHUNK
    tau = jnp.arange(n + 1, dtype=F32)[:, None, None]
    pm = jnp.exp(tau * (a_re * dt))
    pr, pi = pm * jnp.cos(tau * (a_im * dt)), pm * jnp.sin(tau * (a_im * dt))
    e_re = pr[..., None] * bb_re - pi[..., None] * bb_im
    e_im = pr[..., None] * bb_im + pi[..., None] * bb_re
    kern = (jnp.einsum('gcp,tgpd->tgcd', c_re, e_re[:n], precision=hp)
            - jnp.einsum('gcp,tgpd->tgcd', c_im, e_im[:n], precision=hp))
    idx = jnp.arange(n)
    diff = idx[None, :] - idx[:, None]
    kd = jnp.where((diff >= 0)[:, :, None, None, None], kern[jnp.clip(diff, 0, n - 1)], 0.0)
    m_t = kd.transpose(2, 0, 4, 1, 3).reshape(g, n * B_GROUP, n * B_GROUP)
    w_re = e_re[n - 1 - idx].transpose(1, 0, 3, 2)
    w_im = e_im[n - 1 - idx].transpose(1, 0, 3, 2)
    w_t = jnp.concatenate([w_re, w_im], axis=-1).reshape(g, n * B_GROUP, 2 * p)
    f_re = c_re[None] * pr[1:, :, None, :] - c_im[None] * pi[1:, :, None, :]
    f_im = c_re[None] * pi[1:, :, None, :] + c_im[None] * pr[1:, :, None, :]
    v_t = jnp.concatenate([f_re.transpose(1, 3, 0, 2), -f_im.transpose(1, 3, 0, 2)], axis=1)
    v_t = v_t.reshape(g, 2 * p, n * B_GROUP)
    steps = (n * 2.0 ** jnp.arange(levels, dtype=F32))[:, None, None]
    sm = jnp.exp(steps * (a_re * dt))
    sr, si = sm * jnp.cos(steps * (a_im * dt)), sm * jnp.sin(steps * (a_im * dt))
    lam_a = jnp.concatenate([sr, sr], axis=-1)[:, :, None, :]
    lam_b = jnp.concatenate([-si, si], axis=-1)[:, :, None, :]
    d_flat = jnp.tile(d_skip, (1, n))[:, None, :]
    return m_t.astype(BF16), w_t.astype(BF16), v_t.astype(BF16), lam_a, lam_b, d_flat


S5_GROUPS_PER_STEP = LANES // B_GROUP
S5_SCAN_OFF = 128


def _s5_kernel(u_ref, mt_ref, wt_ref, vt_ref, d_ref, la_ref, lb_ref, out_ref, scan_scr, *, nch):
    n = S5_CHUNK
    gps = S5_GROUPS_PER_STEP
    rows = 2 * nch
    levels = la_ref.shape[0]
    lane_blk = _iota((rows, LANES), 1) // B_GROUP
    r = _iota((rows, 1), 0)
    r_in = jnp.where(r >= nch, r - nch, r)
    xs = [u_ref[pl.ds(s, rows, stride=n), :] for s in range(n)]

    def block_transpose(vs):
        for dist in (4, 2, 1):
            low = (lane_blk & dist) == 0
            nxt = list(vs)
            for i in range(gps):
                if i & dist == 0:
                    a, b = vs[i], vs[i + dist]
                    nxt[i] = jnp.where(low, a, pltpu.roll(b, dist * B_GROUP, 1))
                    nxt[i + dist] = jnp.where(low, pltpu.roll(a, LANES - dist * B_GROUP, 1), b)
            vs = nxt
        return vs

    u_half = [block_transpose(xs[half * gps:(half + 1) * gps]) for half in range(2)]
    ys = []
    for g in range(gps):
        u = jnp.concatenate([u_half[0][g], u_half[1][g]], axis=1)
        ub = u.astype(BF16)
        y = _dot(ub, mt_ref[g]) + d_ref[g] * u
        x = _dot(ub, wt_ref[g])
        scan_scr[g, 0:S5_SCAN_OFF, :] = jnp.zeros((S5_SCAN_OFF, LANES), F32)
        for lv in range(levels):
            sh = 1 << lv
            scan_scr[g, S5_SCAN_OFF:S5_SCAN_OFF + rows, :] = x
            prev = scan_scr[g, S5_SCAN_OFF - sh:S5_SCAN_OFF - sh + rows, :]
            prev = jnp.where(r_in >= sh, prev, 0.0)
            x = x + la_ref[lv, g] * prev + lb_ref[lv, g] * pltpu.roll(prev, B_STATE, 1)
        scan_scr[g, S5_SCAN_OFF:S5_SCAN_OFF + rows, :] = x
        x0 = scan_scr[g, S5_SCAN_OFF - 1:S5_SCAN_OFF - 1 + rows, :]
        x0 = jnp.where(r_in >= 1, x0, 0.0)
        ys.append(y + _dot(x0.astype(BF16), vt_ref[g]))
    for half in range(2):
        steps = block_transpose([yg[:, half * LANES:(half + 1) * LANES] for yg in ys])
        for sp in range(gps):
            out_ref[pl.ds(half * gps + sp, rows, stride=n), :] = steps[sp]


def _s5(y, u_col0, params, bsz, lp):
    m_t, w_t, v_t, lam_a, lam_b, d_flat = params
    r = y.shape[0]
    nch = lp // S5_CHUNK
    gps = S5_GROUPS_PER_STEP
    wide = S5_CHUNK * B_GROUP
    st = 2 * B_STATE
    levels = lam_a.shape[0]
    assert bsz % 2 == 0 and (1 << (levels - 1)) <= S5_SCAN_OFF and u_col0 % LANES == 0
    gspec = lambda *shape: pl.BlockSpec((gps,) + shape, lambda j, b: (j, 0, 0))
    lspec = pl.BlockSpec((levels, gps, 1, st), lambda j, b: (0, j, 0, 0))
    return pl.pallas_call(
        functools.partial(_s5_kernel, nch=nch),
        grid=(B_GROUPS // gps, bsz // 2),
        in_specs=[pl.BlockSpec((2 * lp, LANES), lambda j, b: (b, u_col0 // LANES + j)),
                  gspec(wide, wide), gspec(wide, st), gspec(st, wide), gspec(1, wide), lspec, lspec],
        out_specs=pl.BlockSpec((2 * lp, LANES), lambda j, b: (b, j)),
        out_shape=jax.ShapeDtypeStruct((r, B_CH), F32),
        scratch_shapes=[pltpu.VMEM((gps, S5_SCAN_OFF + 2 * nch, LANES), F32)],
        compiler_params=_cparams("parallel", "arbitrary"),
        name="s5",
    )(y, m_t, w_t, v_t, d_flat, lam_a, lam_b)


def _res_ln(h, mix, g, b, pad_rows):
    z = DN_ALPHA * h + mix
    mu = jnp.mean(z, axis=1, keepdims=True)
    zc = z - mu
    var = jnp.mean(zc * zc, axis=1, keepdims=True)
    out = zc * lax.rsqrt(var + LN_EPS) * g + b
    return jnp.where(pad_rows, 0.0, out)


def _pad_rows_mask(tm, lp):
    start = (pl.program_id(0) * tm) % lp
    pos = start + _iota((tm, 1), 0)
    pos = jnp.where(pos >= lp, pos - lp, pos)
    return pos < PAD


def _even_out_kernel(ha_ref, ys_ref, h_ref, wglu_ref, bglu_ref, wout_ref, g_ref, b_ref, out_ref,
                     *, lp):
    tm = h_ref.shape[0]
    yb = _gelu_tanh(ys_ref[...])
    hb = yb * _sigmoid(_dot(yb.astype(BF16), wglu_ref[...]) + bglu_ref[...])
    av = ha_ref.shape[1]
    mix = _dot(ha_ref[...].astype(BF16), wout_ref[:av, :]) + _dot(hb.astype(BF16), wout_ref[av:, :])
    out_ref[...] = _res_ln(h_ref[...], mix, g_ref[...], b_ref[...], _pad_rows_mask(tm, lp))


def _even_out(ha, ys, h, w_glu, b_glu, w_out, ln_g, ln_b, tm, lp):
    r, d = h.shape
    av, bc = ha.shape[1], ys.shape[1]
    full = lambda shape: pl.BlockSpec(shape, lambda i: (0, 0))
    return pl.pallas_call(
        functools.partial(_even_out_kernel, lp=lp),
        grid=(r // tm,),
        in_specs=[pl.BlockSpec((tm, av), lambda i: (i, 0)),
                  pl.BlockSpec((tm, bc), lambda i: (i, 0)),
                  pl.BlockSpec((tm, d), lambda i: (i, 0)),
                  full((bc, bc)), full((1, bc)), full((av + bc, d)), full((1, d)), full((1, d))],
        out_specs=pl.BlockSpec((tm, d), lambda i: (i, 0)),
        out_shape=jax.ShapeDtypeStruct((r, d), F32),
        compiler_params=_cparams("parallel"),
        name="even_out",
    )(ha, ys, h, w_glu, b_glu, w_out, ln_g, ln_b)


def _gdn_kernel(q_ref, k_ref, v_ref, z_ref, gcol_ref, grow_ref, conv_ref, pcol_ref, prow_ref,
                nw_ref, out_ref, s_scr, carry_scr, ext_scr):
    c = pl.program_id(1)

    @pl.when(c == 0)
    def _():
        s_scr[...] = jnp.zeros_like(s_scr)
        carry_scr[...] = jnp.zeros_like(carry_scr)

    n = CHUNK
    ri, ci = _tri_masks(n)
    incl = ri >= ci
    strict = ri > ci
    tri_l = jnp.where(incl, 1.0, 0.0).astype(BF16)
    tri_u = jnp.where(ri <= ci, 1.0, 0.0).astype(BF16)
    first = c == 0
    pad_c = jnp.logical_and(first, _iota((n, 1), 0) < PAD)
    pad_r = jnp.logical_and(first, _iota((1, n), 1) < PAD)

    gc = gcol_ref[...]
    beta_c = jnp.where(pad_c, 0.0, _sigmoid(gc))
    g_c = jnp.where(pad_c, 0.0, pcol_ref[1:2, :] * _softplus(gc + pcol_ref[0:1, :]))
    gcum_c = _dot01_left(tri_l, g_c)
    gr = grow_ref[0]
    g_r = jnp.where(pad_r, 0.0, prow_ref[1] * _softplus(gr + prow_ref[0]))
    gcum_r = _dot01_right(g_r, tri_u)

    width = q_ref.shape[1]
    ext_scr[0:8, :] = carry_scr[...]
    for j, ref in enumerate((q_ref, k_ref, v_ref)):
        ext_scr[8:8 + n, j * width:(j + 1) * width] = ref[...]
    carry_scr[...] = ext_scr[n:n + 8, :]

    def conv(col):
        acc = conv_ref[0:1, col] * ext_scr[5:5 + n, col]
        for j in range(1, C_CONV):
            acc = acc + conv_ref[j:j + 1, col] * ext_scr[5 + j:5 + j + n, col]
        return _silu(acc)

    def l2n(x, scale=1.0):
        return x * (lax.rsqrt(jnp.sum(x * x, axis=1, keepdims=True) + 1e-6) * scale)

    heads = range(C_HEADS)
    lanes = lambda base, h: slice(base + h * LANES, base + (h + 1) * LANES)
    q = [l2n(conv(lanes(0, h)), C_DK ** -0.5) for h in heads]
    k = [l2n(conv(lanes(width, h))) for h in heads]
    v = [conv(lanes(2 * width, h)) for h in heads]
    beta = [beta_c[:, h:h + 1] for h in heads]
    gi = [gcum_c[:, C_HEADS + h:C_HEADS + h + 1] for h in heads]
    gj = [gcum_r[C_HEADS + h:C_HEADS + h + 1, :] for h in heads]
    g_last = [g[n - 1:n, :] for g in gi]
    decay = [jnp.where(incl, jnp.exp(jnp.where(incl, gi[h] - gj[h], 0.0)), 0.0) for h in heads]
    eg = [jnp.exp(g) for g in gi]
    kb = [t.astype(BF16) for t in k]
    qkk = [_dot_nt(jnp.concatenate([q[h].astype(BF16), kb[h]], axis=0), kb[h]) for h in heads]
    attn = [(qkk[h][:n] * decay[h]).astype(BF16) for h in heads]
    ab = [jnp.where(strict, -(beta[h] * qkk[h][n:] * decay[h]), 0.0).astype(BF16) for h in heads]
    x = [jnp.concatenate([beta[h] * v[h], (beta[h] * eg[h]) * k[h]], axis=1) for h in heads]
    x = [x[h] + _dot(ab[h], x[h].astype(BF16)) for h in heads]
    for _ in range(5):
        ab = [_dot(t, t).astype(BF16) for t in ab]
        x = [x[h] + _dot(ab[h], x[h].astype(BF16)) for h in heads]
    s0 = [s_scr[h] for h in heads]
    q_dec = [(q[h] * eg[h]).astype(BF16) for h in heads]
    k_dec = [(k[h] * jnp.exp(g_last[h] - gi[h])).astype(BF16) for h in heads]
    ws_qs = [_dot(jnp.concatenate([x[h][:, LANES:].astype(BF16), q_dec[h]], axis=0), s0[h].astype(BF16))
             for h in heads]
    vb = [(x[h][:, :LANES] - ws_qs[h][:n]).astype(BF16) for h in heads]
    o = [ws_qs[h][n:] + _dot(attn[h], vb[h]) for h in heads]
    for h in heads:
        s_scr[h] = jnp.exp(g_last[h]) * s0[h] + _dot_tn(k_dec[h], vb[h])
    ms = [jnp.mean(t * t, axis=1, keepdims=True) for t in o]
    for h in heads:
        on = o[h] * lax.rsqrt(ms[h] + 1e-6) * nw_ref[...]
        out_ref[:, lanes(0, h)] = on * _silu(z_ref[:, lanes(0, h)])


def _gdn(y, gcol, grow3, conv_w, pcol, prow, norm_w, bsz, nc):
    r = y.shape[0]
    w = C_HEADS * C_DK
    row = lambda b, c: b * nc + c
    return pl.pallas_call(
        _gdn_kernel,
        grid=(bsz, nc),
        in_specs=[pl.BlockSpec((CHUNK, w), lambda b, c: (row(b, c), 0)),
                  pl.BlockSpec((CHUNK, w), lambda b, c: (row(b, c), 1)),
                  pl.BlockSpec((CHUNK, w), lambda b, c: (row(b, c), 2)),
                  pl.BlockSpec((CHUNK, w), lambda b, c: (row(b, c), 3)),
                  pl.BlockSpec((CHUNK, LANES), lambda b, c: (row(b, c), 0)),
                  pl.BlockSpec((1, 16, CHUNK), lambda b, c: (row(b, c), 0, 0)),
                  pl.BlockSpec((C_CONV, 3 * w), lambda b, c: (0, 0)),
                  pl.BlockSpec((2, LANES), lambda b, c: (0, 0)),
                  pl.BlockSpec((2, 16, CHUNK), lambda b, c: (0, 0, 0)),
                  pl.BlockSpec((1, LANES), lambda b, c: (0, 0))],
        out_specs=pl.BlockSpec((CHUNK, w), lambda b, c: (row(b, c), 0)),
        out_shape=jax.ShapeDtypeStruct((r, w), F32),
        scratch_shapes=[pltpu.VMEM((C_HEADS, C_DK, C_DK), F32),
                        pltpu.VMEM((8, 3 * w), F32),
                        pltpu.VMEM((CHUNK + 8, 3 * w), F32)],
        compiler_params=_cparams("parallel", "arbitrary"),
        name="gdn",
    )(y, y, y, y, gcol, grow3, conv_w, pcol, prow, norm_w)


def _odd_out_kernel(o_ref, h_ref, wout_ref, g_ref, b_ref, out_ref, *, lp):
    tm = h_ref.shape[0]
    mix = _dot(o_ref[...].astype(BF16), wout_ref[...])
    out_ref[...] = _res_ln(h_ref[...], mix, g_ref[...], b_ref[...], _pad_rows_mask(tm, lp))


def _odd_out(o, h, w_out, ln_g, ln_b, tm, lp):
    r, d = h.shape
    full = lambda shape: pl.BlockSpec(shape, lambda i: (0, 0))
    return pl.pallas_call(
        functools.partial(_odd_out_kernel, lp=lp),
        grid=(r // tm,),
        in_specs=[pl.BlockSpec((tm, d), lambda i: (i, 0)),
                  pl.BlockSpec((tm, d), lambda i: (i, 0)),
                  full((d, d)), full((1, d)), full((1, d))],
        out_specs=pl.BlockSpec((tm, d), lambda i: (i, 0)),
        out_shape=jax.ShapeDtypeStruct((r, d), F32),
        compiler_params=_cparams("parallel"),
        name="odd_out",
    )(o, h, w_out, ln_g, ln_b)


def _router_kernel(x_ref, w_ref, b_ref, meta_ref, gate_ref, cnt_ref, base_scr):
    @pl.when(pl.program_id(0) == 0)
    def _():
        base_scr[...] = jnp.zeros_like(base_scr)

    tm = x_ref.shape[0]
    logits = _dot(x_ref[...].astype(BF16), w_ref[...]) + b_ref[...]
    lane = _iota((tm, LANES), 1)
    lane_f = lane.astype(F32)
    work = logits
    vals, sels = [], []
    onehot = jnp.zeros((tm, LANES), F32)
    for _ in range(TOP_K):
        m = jnp.max(work, axis=1, keepdims=True)
        idx = jnp.min(jnp.where(work == m, lane_f, float(LANES)), axis=1, keepdims=True)
        sel = lane_f == idx
        vals.append(m)
        sels.append((sel, idx))
        onehot = onehot + jnp.where(sel, 1.0, 0.0)
        work = jnp.where(sel, -jnp.inf, work)
    ri, ci = _tri_masks(tm)
    tri = jnp.where(ri > ci, 1.0, 0.0).astype(BF16)
    before = _dot(tri, onehot.astype(BF16)) + base_scr[0:1, :]
    base_scr[...] = base_scr[...] + jnp.sum(onehot, axis=0, keepdims=True)
    cnt_ref[...] = base_scr[...]
    es = [jnp.exp(v - vals[0]) for v in vals]
    tot = es[0] + es[1] + es[2] + es[3]
    meta = jnp.zeros((tm, LANES), F32)
    gate = jnp.zeros((tm, LANES), F32)
    for k, (sel, idx) in enumerate(sels):
        rank = jnp.sum(jnp.where(sel, before, 0.0), axis=1, keepdims=True)
        meta = jnp.where(lane == k, idx, meta)
        meta = jnp.where(lane == TOP_K + k, rank, meta)
        gate = jnp.where(lane == k, es[k] / tot, gate)
    gate_ref[...] = gate
    pick = jnp.where(_iota((8, LANES), 0) == _iota((8, LANES), 1), 1.0, 0.0).astype(BF16)
    hi, mid, lo = _split3(meta)
    meta_ref[...] = (_dot_nt(pick, hi) + _dot_nt(pick, mid) + _dot_nt(pick, lo)).astype(jnp.int32)


def _router(h, w_r, b_r, tm):
    r, d = h.shape
    return pl.pallas_call(
        _router_kernel,
        grid=(r // tm,),
        in_specs=[pl.BlockSpec((tm, d), lambda i: (i, 0)),
                  pl.BlockSpec((d, LANES), lambda i: (0, 0)),
                  pl.BlockSpec((1, LANES), lambda i: (0, 0))],
        out_specs=[pl.BlockSpec((8, tm), lambda i: (0, i)),
                   pl.BlockSpec((tm, LANES), lambda i: (i, 0)),
                   pl.BlockSpec((8, LANES), lambda i: (0, 0))],
        out_shape=[jax.ShapeDtypeStruct((8, r), jnp.int32),
                   jax.ShapeDtypeStruct((r, LANES), F32),
                   jax.ShapeDtypeStruct((8, LANES), F32)],
        scratch_shapes=[pltpu.VMEM((8, LANES), F32)],
        compiler_params=_cparams("arbitrary"),
        name="router",
    )(h, w_r, b_r)


ROW_DMA_UNROLL = 8
ROW_DMA_WAIT_GROUP = 64


def _issue_row_copies(make_copy, tm):
    def issue(i, carry):
        for u in range(ROW_DMA_UNROLL):
            for k in range(TOP_K):
                make_copy(i * ROW_DMA_UNROLL + u, k).start(priority=(u * TOP_K + k) % 2)
        return carry

    lax.fori_loop(0, tm // ROW_DMA_UNROLL, issue, 0)


def _wait_row_copies(make_copy, tm):
    def drain(i, carry):
        for _ in range(ROW_DMA_WAIT_GROUP):
            make_copy(0, 0).wait()
        return carry

    lax.fori_loop(0, tm * TOP_K // ROW_DMA_WAIT_GROUP, drain, 0)


def _dispatch_kernel(dest_ref, x_hbm, xs_in, xs_hbm, stage, sem_in, sem_out, *, tm):
    del xs_in
    i = pl.program_id(0)
    n = pl.num_programs(0)
    cur = i % 3

    def load(tile, into):
        return pltpu.make_async_copy(x_hbm.at[pl.ds(tile * tm, tm)], stage.at[into], sem_in.at[into])

    def scatter(src_slot, sem_slot):
        def make_copy(t, k):
            return pltpu.make_async_copy(stage.at[src_slot, pl.ds(t, 1)],
                                         xs_hbm.at[pl.ds(dest_ref[k * tm + t], 1)], sem_out.at[sem_slot])
        return make_copy

    @pl.when(i == 0)
    def _():
        load(0, 0).start()

    @pl.when(i + 1 < n)
    def _():
        load(i + 1, (i + 1) % 3).start()

    load(i, cur).wait()
    _issue_row_copies(scatter(cur, i % 2), tm)

    @pl.when(i >= 1)
    def _():
        _wait_row_copies(scatter((i + 2) % 3, (i + 1) % 2), tm)

    @pl.when(i == n - 1)
    def _():
        _wait_row_copies(scatter(cur, i % 2), tm)


def _dispatch(h, dest_flat, xs0, tm):
    r, d = h.shape
    n_slots = xs0.shape[0]
    return pl.pallas_call(
        functools.partial(_dispatch_kernel, tm=tm),
        grid=(r // tm,),
        in_specs=[pl.BlockSpec((tm * TOP_K,), lambda i: (i,), memory_space=pltpu.SMEM),
                  pl.BlockSpec(memory_space=pl.ANY),
                  pl.BlockSpec(memory_space=pl.ANY)],
        out_specs=pl.BlockSpec(memory_space=pl.ANY),
        out_shape=jax.ShapeDtypeStruct((n_slots, d), h.dtype),
        scratch_shapes=[pltpu.VMEM((3, tm, d), h.dtype), pltpu.SemaphoreType.DMA((3,)),
                        pltpu.SemaphoreType.DMA((2,))],
        input_output_aliases={2: 0},
        compiler_params=_cparams("arbitrary"),
        name="moe_dispatch",
    )(dest_flat, h, xs0)


EXPERT_BLOCKS_PER_STEP = 2


def _expert_kernel(be_ref, nu_ref, slot_ref, nxt_ref, x_ref, bgu_ref, bd_ref, wgu_hbm, wd_hbm, out_ref,
                   wgu_buf, wd_buf, wgu_scr, wd_scr, sems, *, layer):
    tile = 2 * LANES
    n_tiles = wgu_buf.shape[2] // tile

    def weight_copies(expert, into):
        return (pltpu.make_async_copy(wgu_hbm.at[layer, expert], wgu_buf.at[into], sems.at[0, into]),
                pltpu.make_async_copy(wd_hbm.at[layer, expert], wd_buf.at[into], sems.at[1, into]))

    def one_block(b, rows):
        used = b < nu_ref[0]
        e = be_ref[b]
        fresh = jnp.logical_and(used, jnp.logical_or(b == 0, e != be_ref[jnp.maximum(b - 1, 0)]))
        slot = slot_ref[b]

        @pl.when(jnp.logical_and(used, b == 0))
        def _():
            for cp in weight_copies(e, slot):
                cp.start()

        @pl.when(fresh)
        def _():
            for cp in weight_copies(e, slot):
                cp.wait()

            @pl.when(nxt_ref[b] >= 0)
            def _():
                for cp in weight_copies(nxt_ref[b], 1 - slot):
                    cp.start()

            r = _iota((tile, tile), 0)
            c = _iota((tile, tile), 1)
            src = jnp.where(c < LANES, 2 * c, 2 * (c - LANES) + 1)
            perm = jnp.where(r == src, 1.0, 0.0).astype(BF16)
            for j in range(n_tiles):
                cols = slice(j * tile, (j + 1) * tile)
                wgu_scr[:, cols] = _dot(wgu_buf[slot, :, cols].astype(BF16), perm).astype(BF16)
            wd_scr[...] = wd_buf[slot].astype(BF16)

        @pl.when(used)
        def _():
            xb = x_ref[rows, :].astype(BF16)
            h = _dot(xb, wgu_scr[...]) + bgu_ref[e]
            acts = []
            for j in range(n_tiles):
                gate = jnp.minimum(h[:, j * tile:j * tile + LANES], SWIGLU_LIMIT)
                up = jnp.clip(h[:, j * tile + LANES:(j + 1) * tile], -SWIGLU_LIMIT, SWIGLU_LIMIT)
                acts.append(((up + 1.0) * gate * _sigmoid(SWIGLU_ALPHA * gate)).astype(BF16))
            act = jnp.concatenate(acts, axis=1)
            out_ref[rows, :] = _dot(act, wd_scr[...]) + bd_ref[e]

        @pl.when(jnp.logical_not(used))
        def _():
            out_ref[rows, :] = jnp.zeros((MOE_BLOCK, out_ref.shape[1]), out_ref.dtype)

    for s in range(EXPERT_BLOCKS_PER_STEP):
        one_block(pl.program_id(0) * EXPERT_BLOCKS_PER_STEP + s, slice(s * MOE_BLOCK, (s + 1) * MOE_BLOCK))


def _experts(xs, block_e, n_used, slot_of, next_e, w_gu_all, b_gu, w_d_all, b_d, layer):
    n_slots, d = xs.shape
    n_blocks = n_slots // MOE_BLOCK
    de2 = w_gu_all.shape[3]
    de = w_d_all.shape[2]
    rows = EXPERT_BLOCKS_PER_STEP * MOE_BLOCK
    assert n_blocks % EXPERT_BLOCKS_PER_STEP == 0
    grid_spec = pltpu.PrefetchScalarGridSpec(
        num_scalar_prefetch=4,
        grid=(n_blocks // EXPERT_BLOCKS_PER_STEP,),
        in_specs=[pl.BlockSpec((rows, d), lambda i, be, nu, sl, nx: (i, 0)),
                  pl.BlockSpec((N_EXPERTS, 1, de2), lambda i, be, nu, sl, nx: (0, 0, 0)),
                  pl.BlockSpec((N_EXPERTS, 1, d), lambda i, be, nu, sl, nx: (0, 0, 0)),
                  pl.BlockSpec(memory_space=pl.ANY),
                  pl.BlockSpec(memory_space=pl.ANY)],
        out_specs=pl.BlockSpec((rows, d), lambda i, be, nu, sl, nx: (i, 0)),
        scratch_shapes=[pltpu.VMEM((2, d, de2), F32), pltpu.VMEM((2, de, d), F32),
                        pltpu.VMEM((d, de2), BF16), pltpu.VMEM((de, d), BF16),
                        pltpu.SemaphoreType.DMA((2, 2))],
    )
    return pl.pallas_call(
        functools.partial(_expert_kernel, layer=layer),
        grid_spec=grid_spec,
        out_shape=jax.ShapeDtypeStruct((n_slots, d), F32),
        compiler_params=_cparams("arbitrary"),
        name="moe_experts",
    )(block_e, n_used, slot_of, next_e, xs, b_gu, b_d, w_gu_all, w_d_all)


def _combine_kernel(dest_ref, dest_next_ref, gate_ref, h_ref, g_ref, b_ref, eo_hbm, out_ref, buf, sems,
                    *, lp):
    tm = h_ref.shape[0]
    i = pl.program_id(0)
    slot = i % 2

    def gather(idx_ref, into):
        def make_copy(t, k):
            return pltpu.make_async_copy(eo_hbm.at[pl.ds(idx_ref[k * tm + t], 1)],
                                         buf.at[into, k, pl.ds(t, 1)], sems.at[into])
        return make_copy

    @pl.when(i == 0)
    def _():
        _issue_row_copies(gather(dest_ref, slot), tm)

    @pl.when(i + 1 < pl.num_programs(0))
    def _():
        _issue_row_copies(gather(dest_next_ref, 1 - slot), tm)

    _wait_row_copies(gather(dest_ref, slot), tm)

    gates = gate_ref[...]
    ffn = gates[:, 0:1] * buf[slot, 0]
    for k in range(1, TOP_K):
        ffn = ffn + gates[:, k:k + 1] * buf[slot, k]
    out_ref[...] = _res_ln(h_ref[...], ffn, g_ref[...], b_ref[...], _pad_rows_mask(tm, lp))


def _combine(eo, dest_flat, gates, h, ln_g, ln_b, tm, lp):
    r, d = h.shape
    n_tiles = r // tm
    full = lambda shape: pl.BlockSpec(shape, lambda i: (0, 0))
    return pl.pallas_call(
        functools.partial(_combine_kernel, lp=lp),
        grid=(n_tiles,),
        in_specs=[pl.BlockSpec((tm * TOP_K,), lambda i: (i,), memory_space=pltpu.SMEM),
                  pl.BlockSpec((tm * TOP_K,), lambda i: (jnp.minimum(i + 1, n_tiles - 1),),
                               memory_space=pltpu.SMEM),
                  pl.BlockSpec((tm, LANES), lambda i: (i, 0)),
                  pl.BlockSpec((tm, d), lambda i: (i, 0)),
                  full((1, d)), full((1, d)),
                  pl.BlockSpec(memory_space=pl.ANY)],
        out_specs=pl.BlockSpec((tm, d), lambda i: (i, 0)),
        out_shape=jax.ShapeDtypeStruct((r, d), F32),
        scratch_shapes=[pltpu.VMEM((2, TOP_K, tm, d), F32), pltpu.SemaphoreType.DMA((2,))],
        compiler_params=_cparams("arbitrary"),
        name="moe_combine",
    )(dest_flat, dest_flat, gates, h, ln_g, ln_b, eo)


def _moe(h, w_router, b_router, w_gu_all, b_gu, w_d_all, b_d, ln_g, ln_b, tm, lp, xs_prev, layer):
    r, d = h.shape
    w_r = jnp.pad(w_router, ((0, 0), (0, LANES - N_EXPERTS))).astype(BF16)
    b_r = jnp.pad(b_router, (0, LANES - N_EXPERTS), constant_values=NEG)[None, :]
    meta, gates, cnt = _router(h, w_r, b_r, 2 * tm if r % (2 * tm) == 0 else tm)
    counts = cnt[0, :N_EXPERTS].astype(jnp.int32)
    padded = (counts + MOE_BLOCK - 1) // MOE_BLOCK * MOE_BLOCK
    ends_p = jnp.cumsum(padded)
    pstart = ends_p - padded
    n_blocks = -(-(r * TOP_K) // MOE_BLOCK) + N_EXPERTS
    n_slots = n_blocks * MOE_BLOCK
    block_start = jnp.arange(n_blocks, dtype=jnp.int32) * MOE_BLOCK
    block_e = jnp.minimum(jnp.sum((ends_p[None, :] <= block_start[:, None]).astype(jnp.int32), axis=1),
                          N_EXPERTS - 1)
    n_used = (ends_p[-1:] // MOE_BLOCK).astype(jnp.int32)
    ids = jnp.arange(N_EXPERTS, dtype=jnp.int32)
    first_slot = jnp.sum(jnp.where(meta[None, :TOP_K] == ids[:, None, None], pstart[:, None, None], 0), axis=0)
    dest = first_slot + meta[TOP_K:2 * TOP_K]
    dest = dest.reshape(TOP_K, r // tm, tm).transpose(1, 0, 2).reshape(-1)
    nonempty = padded > 0
    seg =jnp.cumsum(nonempty.astype(jnp.int32)) - nonempty.astype(jnp.int32)
    later = jnp.min(jnp.where(jnp.logical_and(nonempty[None, :], ids[None, :] > ids[:, None]),
                              ids[None, :], N_EXPERTS), axis=1)
    slot_of = (seg % 2)[block_e]
    next_e = jnp.where(later == N_EXPERTS, -1, later)[block_e]
    xs = _dispatch(h, dest, jnp.zeros((n_slots, d), h.dtype) if xs_prev is None else xs_prev, tm)
    half = w_gu_all.shape[3] // 2
    b_gu_t = b_gu.reshape(N_EXPERTS, half // LANES, LANES, 2).transpose(0, 1, 3, 2)
    b_gu_t = b_gu_t.reshape(N_EXPERTS, 1, 2 * half)
    eo = _experts(xs, block_e, n_used, slot_of, next_e, w_gu_all, b_gu_t, w_d_all, b_d[:, None, :], layer)
    return _combine(eo, dest, gates, h, ln_g, ln_b, tm, lp), xs


def _pad_heads(w, heads, dim):
    k = w.shape[0]
    return jnp.pad(w.reshape(k, heads, dim), ((0, 0), (0, 0), (0, LANES - dim))).reshape(k, heads * LANES)


def _gate_weights(wg):
    k, n = wg.shape
    return (jnp.pad(wg, ((0, 0), (0, LANES - n))).astype(BF16),
            jnp.pad(wg.T, ((0, 16 - n), (0, 0))).astype(BF16))


def _even_layer(h, p, bsz, nc, tm, lp):
    r = h.shape[0]
    w_in = p['w_in']
    aq, av = A_HEADS * A_DQK, A_HEADS * A_DV
    o0 = 2 * aq + 2 * av
    w_main = jnp.concatenate([_pad_heads(w_in[:, :aq], A_HEADS, A_DQK),
                              _pad_heads(w_in[:, aq:2 * aq], A_HEADS, A_DQK),
                              w_in[:, 2 * aq:2 * aq + av], w_in[:, 2 * aq + av:o0],
                              w_in[:, o0 + 2 * A_HEADS:]], axis=1).astype(BF16)
    w_gate, w_gate_t = _gate_weights(w_in[:, o0:o0 + 2 * A_HEADS])
    y, gcol, grow = _in_proj(h, w_main, w_gate, w_gate_t, _chunk_tile(r // bsz, 704), w_main.shape[1])
    gb = p['gate_bias']
    bias_col = jnp.pad(gb, (0, LANES - gb.shape[0]))[None, :]
    bias_row = jnp.broadcast_to(jnp.pad(gb, (0, 16 - gb.shape[0]))[:, None], (16, CHUNK))
    ha = _mlstm(y, gcol, grow, bias_col, bias_row, p['head_norm'][None, :], bsz, nc)
    levels = max(1, math.ceil(math.log2(lp // S5_CHUNK)))
    ys = _s5(y, 4 * A_HEADS * LANES,
             _s5_params(p['a_re'], p['a_im'], p['log_step'], p['b_re'], p['b_im'],
                        p['c_re'], p['c_im'], p['d'], levels), bsz, lp)
    return _even_out(ha, ys, h, p['w_glu'].astype(BF16), p['b_glu'][None, :], p['w_out'].astype(BF16),
                     p['ln_g'][None, :], p['ln_b'][None, :], tm, lp)


def _odd_layer(h, p, bsz, nc, tm, lp):
    w_in = p['w_in']
    cw = C_HEADS * C_DK
    w_main = w_in[:, :4 * cw].astype(BF16)
    w_gate, w_gate_t = _gate_weights(w_in[:, 4 * cw:])
    y, gcol, grow = _in_proj(h, w_main, w_gate, w_gate_t, _chunk_tile(h.shape[0] // bsz, 704),
                             w_main.shape[1])
    neg_a = -jnp.exp(p['a_log'])
    zeros = jnp.zeros((C_HEADS,), F32)
    dt16 = jnp.concatenate([zeros, p['dt_bias']])
    na16 = jnp.concatenate([zeros, neg_a])
    pcol = jnp.pad(jnp.stack([dt16, na16]), ((0, 0), (0, LANES - 16)))
    prow = jnp.broadcast_to(jnp.stack([dt16, na16])[:, :, None], (2, 16, CHUNK))
    o = _gdn(y, gcol, grow, p['conv'], pcol, prow, p['norm'][None, :], bsz, nc)
    return _odd_out(o, h, p['w_out'].astype(BF16), p['ln_g'][None, :], p['ln_b'][None, :], tm, lp)


def kernel(x, meta_tokens, ln_g, ln_b, ev_w_in, ev_gate_bias, ev_head_norm, s5_a_re, s5_a_im, s5_log_step, s5_b_re, s5_b_im, s5_c_re, s5_c_im, s5_d, s5_w_glu, s5_b_glu, ev_w_out, od_w_in, od_conv, od_a_log, od_dt_bias, od_norm, od_w_out, moe_w_router, moe_b_router, moe_w_gate_up, moe_b_gate_up, moe_w_down, moe_b_down):
    bsz, seq, d = x.shape
    lp = PAD + N_META + seq
    assert lp % CHUNK == 0 and d == D_MODEL
    nc = lp // CHUNK
    r = bsz * lp
    assert r % MOE_BLOCK == 0 and lp >= MOE_BLOCK
    tm = 512 if (r % 512 == 0 and lp >= 512) else MOE_BLOCK
    meta = jnp.broadcast_to(meta_tokens[None], (bsz, N_META, d)).astype(x.dtype)
    h = jnp.concatenate([jnp.zeros((bsz, PAD, d), x.dtype), meta, x], axis=1).reshape(bsz * lp, d)
    xs = None
    for layer in range(ln_g.shape[0]):
        j = layer // 2
        if layer % 2 == 0:
            p = dict(w_in=ev_w_in[j], gate_bias=ev_gate_bias[j], head_norm=ev_head_norm[j],
                     a_re=s5_a_re[j], a_im=s5_a_im[j], log_step=s5_log_step[j], b_re=s5_b_re[j],
                     b_im=s5_b_im[j], c_re=s5_c_re[j], c_im=s5_c_im[j], d=s5_d[j],
                     w_glu=s5_w_glu[j], b_glu=s5_b_glu[j], w_out=ev_w_out[j],
                     ln_g=ln_g[layer, 0], ln_b=ln_b[layer, 0])
            h = _even_layer(h, p, bsz, nc, tm, lp)
        else:
            p = dict(w_in=od_w_in[j], conv=od_conv[j], a_log=od_a_log[j], dt_bias=od_dt_bias[j],
                     norm=od_norm[j], w_out=od_w_out[j], ln_g=ln_g[layer, 0], ln_b=ln_b[layer, 0])
            h = _odd_layer(h, p, bsz, nc, tm, lp)
        h, xs = _moe(h, moe_w_router[layer], moe_b_router[layer], moe_w_gate_up,
                     moe_b_gate_up[layer], moe_w_down, moe_b_down[layer],
                     ln_g[layer, 1][None, :], ln_b[layer, 1][None, :], MOE_BLOCK, lp, xs, layer)
    return h.reshape(bsz, lp, d)[:, PAD + N_META:]
```
